```python
import jax, jax.numpy as jnp
from jax import lax
import numpy as np

D_MODEL = 4096
BATCH = 2
SEQ = 4096
DEPTH = 2

HEAD_DIM = 128
D_MIX = D_MODEL
D_ATTN = D_MIX // 2
D_SGU = D_MIX - D_ATTN
N_ATTN_HEADS = D_ATTN // HEAD_DIM
SGU_GROUP_DIM = 128
N_SGU_GROUPS = D_SGU // SGU_GROUP_DIM
SGU_CHUNK = 128
DILATED_CONFIGS = ((128, 1), (512, 4), (2048, 16))
BLOCK = 128
N_IN = 3 * D_ATTN + 2 * D_SGU
D_FF = 256 * ((8 * D_MODEL // 3 + 255) // 256)
CONV_WIDTH = 3
NORM_GROUP = 128
NORM_EPS = 1e-6

kernel_name = "hybrid_dilated_attn_gmlp_convffn"


def rmsnorm(x, g):
    xf = x.astype(jnp.float32)
    y = xf * lax.rsqrt(jnp.mean(xf * xf, axis=-1, keepdims=True) + NORM_EPS)
    return (y * g.astype(jnp.float32)).astype(x.dtype)


def group_rmsnorm(x, g, group):
    shp = x.shape
    xf = x.astype(jnp.float32).reshape(shp[:-1] + (shp[-1] // group, group))
    y = xf * lax.rsqrt(jnp.mean(xf * xf, axis=-1, keepdims=True) + NORM_EPS)
    return (y.reshape(shp) * g.astype(jnp.float32)).astype(x.dtype)


def alibi_slopes(n):
    return 2.0 ** (-8.0 * jnp.arange(1, n + 1, dtype=jnp.float32) / n)


def dilated_window_branch(q, k, v, slopes, window, dilation):
    B, H, S, E = q.shape
    d = dilation
    span = window // d
    L = S // d
    nb = -(-L // BLOCK)
    lp = nb * BLOCK

    def strided(t):
        t = t.reshape(B, H, L, d, E).transpose(0, 1, 3, 2, 4)
        t = jnp.pad(t, ((0, 0), (0, 0), (0, 0), (0, lp - L), (0, 0)))
        return t.reshape(B, H, d, nb, BLOCK, E)

    def with_prev(t):
        prev = jnp.pad(t[:, :, :, :-1], ((0, 0), (0, 0), (0, 0), (1, 0), (0, 0), (0, 0)))
        return jnp.concatenate([prev, t], axis=4)

    qb = strided(q)
    kk = with_prev(strided(k))
    vv = with_prev(strided(v))

    s = jnp.einsum('bhrnqe,bhrnke->bhrnqk', qb, kk).astype(jnp.float32) * (HEAD_DIM ** -0.5)
    qi = jnp.arange(BLOCK)[:, None]
    kj = jnp.arange(2 * BLOCK)[None, :]
    dist = qi + BLOCK - kj
    key_idx = jnp.arange(nb)[:, None, None] * BLOCK + kj[None] - BLOCK
    valid = (dist >= 0) & (dist <= span) & (key_idx >= 0)
    bias = -slopes[:, None, None, None, None] * (dist * d).astype(jnp.float32)[None, None, None]
    s = jnp.where(valid, s + bias, jnp.finfo(jnp.float32).min)
    lse = jax.nn.logsumexp(s, axis=-1)
    p = jnp.exp(s - lse[..., None])
    o = jnp.einsum('bhrnqk,bhrnke->bhrnqe', p, vv.astype(jnp.float32))

    def unstrided(t):
        rest = t.shape[5:]
        t = t.reshape((B, H, d, lp) + rest)[:, :, :, :L]
        t = jnp.swapaxes(t, 2, 3)
        return t.reshape((B, H, S) + rest)

    return unstrided(o), unstrided(lse)


def dilated_attention(q, k, v):
    slopes = alibi_slopes(q.shape[1])
    outs, lses = [], []
    for window, dilation in DILATED_CONFIGS:
        o, lse = dilated_window_branch(q, k, v, slopes, window, dilation)
        outs.append(o)
        lses.append(lse)
    wts = jax.nn.softmax(jnp.stack(lses, axis=0), axis=0)
    o = jnp.einsum('cbhs,cbhse->bhse', wts, jnp.stack(outs, axis=0))
    return o.astype(q.dtype)


def spatial_gating(z, norm_g, w_s, b_s):
    B, S, _ = z.shape
    u, v = jnp.split(z, 2, axis=-1)
    v = group_rmsnorm(v, norm_g, SGU_GROUP_DIM)
    v = v.reshape(B, S // SGU_CHUNK, SGU_CHUNK, N_SGU_GROUPS, SGU_GROUP_DIM)
    w = jnp.tril(w_s)
    mixed = jnp.einsum('gts,bnsgc->bntgc', w, v) + b_s.T[None, None, :, :, None]
    return u * mixed.reshape(B, S, D_SGU)


def mixing_sublayer(h, w_in, sgu_norm_g, w_spatial, b_spatial, mix_norm_g, w_out):
    B, S, _ = h.shape
    z = h @ w_in
    q, k, v, zg = jnp.split(z, [D_ATTN, 2 * D_ATTN, 3 * D_ATTN], axis=-1)

    def heads(t):
        return t.reshape(B, S, N_ATTN_HEADS, HEAD_DIM).transpose(0, 2, 1, 3)

    a = dilated_attention(heads(q), heads(k), heads(v))
    a = a.transpose(0, 2, 1, 3).reshape(B, S, D_ATTN)
    g = spatial_gating(jax.nn.gelu(zg, approximate=False), sgu_norm_g, w_spatial, b_spatial)
    m = jnp.concatenate([a, g], axis=-1)
    m = group_rmsnorm(m, mix_norm_g, NORM_GROUP)
    return m @ w_out


def causal_depthwise_conv(h, w, b):
    S = h.shape[1]
    hp = jnp.pad(h, ((0, 0), (CONV_WIDTH - 1, 0), (0, 0)))
    y = b + w[0] * hp[:, 0:S]
    for j in range(1, CONV_WIDTH):
        y = y + w[j] * hp[:, j:j + S]
    return y


def conv_ffn(h, w_up, conv_w, conv_b, w_down):
    up = causal_depthwise_conv(h @ w_up, conv_w, conv_b)
    gate, val = jnp.split(up, 2, axis=-1)
    return (jax.nn.silu(gate) * val) @ w_down


def setup_inputs(seed: int = 0) -> dict:
    key = jax.random.key(seed)
    ks = jax.random.split(key, 16)
    f32 = jnp.float32
    L = DEPTH
    nrm = lambda k, shp, s: jax.random.normal(k, shp, f32) * s
    return {
        "x": jax.random.normal(ks[0], (BATCH, SEQ, D_MODEL), f32),
        "attn_norm_g": 1.0 + nrm(ks[1], (L, D_MODEL), 0.02),
        "w_in": nrm(ks[2], (L, D_MODEL, N_IN), D_MODEL ** -0.5),
        "sgu_norm_g": 1.0 + nrm(ks[3], (L, D_SGU), 0.02),
        "w_spatial": nrm(ks[4], (L, N_SGU_GROUPS, SGU_CHUNK, SGU_CHUNK), SGU_CHUNK ** -0.5),
        "b_spatial": nrm(ks[5], (L, N_SGU_GROUPS, SGU_CHUNK), 0.02),
        "mix_norm_g": 1.0 + nrm(ks[6], (L, D_MIX), 0.02),
        "w_out": nrm(ks[7], (L, D_MIX, D_MODEL), D_MIX ** -0.5),
        "ffn_norm_g": 1.0 + nrm(ks[8], (L, D_MODEL), 0.02),
        "w_up": nrm(ks[9], (L, D_MODEL, 2 * D_FF), D_MODEL ** -0.5),
        "conv_w": nrm(ks[10], (L, CONV_WIDTH, 2 * D_FF), CONV_WIDTH ** -0.5),
        "conv_b": nrm(ks[11], (L, 2 * D_FF), 0.02),
        "w_down": nrm(ks[12], (L, D_FF, D_MODEL), D_FF ** -0.5),
        "final_norm_g": 1.0 + nrm(ks[13], (D_MODEL,), 0.02),
    }


def reference(x, attn_norm_g, w_in, sgu_norm_g, w_spatial, b_spatial, mix_norm_g, w_out,
              ffn_norm_g, w_up, conv_w, conv_b, w_down, final_norm_g):
    for i in range(DEPTH):
        h = rmsnorm(x, attn_norm_g[i])
        x = x + mixing_sublayer(h, w_in[i], sgu_norm_g[i], w_spatial[i], b_spatial[i],
                                mix_norm_g[i], w_out[i])
        h = rmsnorm(x, ffn_norm_g[i])
        x = x + conv_ffn(h, w_up[i], conv_w[i], conv_b[i], w_down[i])
    return rmsnorm(x, final_norm_g)
```

```python
import functools
import math

import jax
import jax.numpy as jnp
from jax import lax
from jax.experimental import pallas as pl
from jax.experimental.pallas import tpu as pltpu

F32 = jnp.float32
BF16 = jnp.bfloat16

D_MODEL = 4096
HEAD_DIM = 128
D_ATTN = D_MODEL // 2
D_SGU = D_MODEL - D_ATTN
N_HEADS = D_ATTN // HEAD_DIM
N_GROUPS = D_SGU // HEAD_DIM
CHUNK = 128
BLOCK = 128
DILATIONS = (1, 4, 16)
N_IN = 3 * D_ATTN + 2 * D_SGU
D_FF = 256 * ((8 * D_MODEL // 3 + 255) // 256)
CONV_WIDTH = 3
EPS = 1e-6
VMEM_LIMIT = 56 * 1024 * 1024
CARRY_ROWS = 8


def _params(*sem):
    return pltpu.CompilerParams(dimension_semantics=sem, vmem_limit_bytes=VMEM_LIMIT)


def _rmsnorm_body(x_ref, g_ref, o_ref):
    x = x_ref[...]
    y = x * lax.rsqrt(jnp.mean(x * x, axis=-1, keepdims=True) + EPS)
    o_ref[...] = (y * g_ref[...]).astype(o_ref.dtype)


def _rmsnorm(x, g, out_dtype, tm=256):
    m, d = x.shape
    return pl.pallas_call(
        _rmsnorm_body,
        grid=(m // tm,),
        in_specs=[pl.BlockSpec((tm, d), lambda i: (i, 0)),
                  pl.BlockSpec((1, d), lambda i: (0, 0))],
        out_specs=pl.BlockSpec((tm, d), lambda i: (i, 0)),
        out_shape=jax.ShapeDtypeStruct((m, d), out_dtype),
        compiler_params=_params("parallel"),
        name="rmsnorm",
    )(x, g.reshape(1, d))


def _inproj_body(h_ref, w_ref, o_ref):
    o_ref[...] = jnp.dot(h_ref[...], w_ref[...], preferred_element_type=F32).astype(o_ref.dtype)


def _inproj(h, w, tm=1024, tn=1024):
    m, k = h.shape
    n = w.shape[1]
    return pl.pallas_call(
        _inproj_body,
        grid=(m // tm, n // tn),
        in_specs=[pl.BlockSpec((tm, k), lambda i, j: (i, 0)),
                  pl.BlockSpec((k, tn), lambda i, j: (0, j))],
        out_specs=pl.BlockSpec((tm, tn), lambda i, j: (i, j)),
        out_shape=jax.ShapeDtypeStruct((m, n), BF16),
        compiler_params=_params("parallel", "arbitrary"),
        name="inproj",
    )(h, w)


def _attn_body(q_ref, k_ref, v_ref, slope_ref, g_ref, o_ref, qf, kf, vf, ob, lb, *, seq):
    qf[...] = q_ref[...].astype(F32)
    kf[...] = k_ref[...].astype(F32)
    vf[...] = v_ref[...].astype(F32)
    slope = slope_ref[...]
    qi = lax.broadcasted_iota(jnp.int32, (BLOCK, 2 * BLOCK), 0)
    kj = lax.broadcasted_iota(jnp.int32, (BLOCK, 2 * BLOCK), 1)
    dist = qi + BLOCK - kj
    valid = (dist >= 0) & (dist <= BLOCK)
    neg = jnp.finfo(F32).min
    scale = HEAD_DIM ** -0.5

    for c, d in enumerate(DILATIONS):
        nb = seq // d // BLOCK
        bias = -slope[:, :1] * (dist * d).astype(F32)

        def block(r, n, first, c=c, d=d, bias=bias):
            q0 = r + d * BLOCK * n
            rows = pl.ds(q0, BLOCK, stride=d)
            q = qf[rows, :].astype(BF16)
            if first:
                krows = rows
                b_, v_ = bias[:, BLOCK:], valid[:, BLOCK:]
            else:
                krows = pl.ds(q0 - d * BLOCK, 2 * BLOCK, stride=d)
                b_, v_ = bias, valid
            k = kf[krows, :].astype(BF16)
            v = vf[krows, :].astype(BF16)
            s = lax.dot_general(q, k, (((1,), (1,)), ((), ())), preferred_element_type=F32)
            s = jnp.where(v_, s * scale + b_, neg)
            mx = jnp.max(s, axis=-1, keepdims=True)
            p = jnp.exp(s - mx)
            l = jnp.sum(p, axis=-1, keepdims=True)
            o = jnp.dot(p.astype(BF16), v, preferred_element_type=F32) / l
            ob[c, rows, :] = o
            lb[c, rows, :] = jnp.broadcast_to(mx + jnp.log(l), (BLOCK, HEAD_DIM))

        def r_body(r, carry, block=block, nb=nb):
            block(r, 0, True)

            def n_body(n, carry2):
                block(r, n, False)
                return carry2

            return lax.fori_loop(1, nb, n_body, carry)

        lax.fori_loop(0, d, r_body, 0)

    g = g_ref[...]
    step = 256

    def fin(t, carry):
        rows = pl.ds(pl.multiple_of(t * step, step), step)
        l0, l1, l2 = lb[0, rows, :], lb[1, rows, :], lb[2, rows, :]
        mx = jnp.maximum(jnp.maximum(l0, l1), l2)
        e0, e1, e2 = jnp.exp(l0 - mx), jnp.exp(l1 - mx), jnp.exp(l2 - mx)
        den = e0 + e1 + e2
        a = (e0 / den) * ob[0, rows, :] + (e1 / den) * ob[1, rows, :] + (e2 / den) * ob[2, rows, :]
        y = a * lax.rsqrt(jnp.mean(a * a, axis=-1, keepdims=True) + EPS)
        o_ref[rows, :] = (y * g).astype(o_ref.dtype)
        return carry

    lax.fori_loop(0, seq // step, fin, 0)


def _attention(z3, slopes, mix_g):
    b, s, _ = z3.shape
    blk = lambda off: pl.BlockSpec((None, s, HEAD_DIM), lambda bi, h, off=off: (bi, 0, off + h))
    return pl.pallas_call(
        functools.partial(_attn_body, seq=s),
        grid=(b, N_HEADS),
        in_specs=[blk(0), blk(N_HEADS), blk(2 * N_HEADS),
                  pl.BlockSpec((None, 1, HEAD_DIM), lambda bi, h: (h, 0, 0)),
                  pl.BlockSpec((1, HEAD_DIM), lambda bi, h: (0, h))],
        out_specs=pl.BlockSpec((None, s, HEAD_DIM), lambda bi, h: (bi, 0, h)),
        out_shape=jax.ShapeDtypeStruct((b, s, D_ATTN), BF16),
        scratch_shapes=[pltpu.VMEM((s, HEAD_DIM), F32)] * 3
        + [pltpu.VMEM((3, s, HEAD_DIM), F32)] * 2,
        compiler_params=_params("parallel", "parallel"),
        name="dilated_attn",
    )(z3, z3, z3, slopes, mix_g)


def _gelu(x):
    return 0.5 * x * (1.0 + lax.erf(x * math.sqrt(0.5)))


def _sgu_body(u_ref, v_ref, ng_ref, w_ref, bt_ref, mg_ref, o_ref, *, tm):
    ti = lax.broadcasted_iota(jnp.int32, (CHUNK, CHUNK), 0)
    si = lax.broadcasted_iota(jnp.int32, (CHUNK, CHUNK), 1)
    causal = si <= ti
    for g in range(N_GROUPS):
        cols = slice(g * HEAD_DIM, (g + 1) * HEAD_DIM)
        w = jnp.where(causal, w_ref[g], 0.0).astype(BF16)
        bcol = bt_ref[:, g:g + 1]
        ng = ng_ref[:, cols]
        mg = mg_ref[:, cols]
        for ch in range(tm // CHUNK):
            rows = slice(ch * CHUNK, (ch + 1) * CHUNK)
            u = _gelu(u_ref[rows, cols].astype(F32))
            v = _gelu(v_ref[rows, cols].astype(F32))
            v = v * lax.rsqrt(jnp.mean(v * v, axis=-1, keepdims=True) + EPS) * ng
            mixed = jnp.dot(w, v.astype(BF16), preferred_element_type=F32) + bcol
            y = u * mixed
            y = y * lax.rsqrt(jnp.mean(y * y, axis=-1, keepdims=True) + EPS) * mg
            o_ref[rows, cols] = y.astype(o_ref.dtype)


def _sgu(z, sgu_norm_g, w_spatial, b_spatial_t, mix_g, tm=256):
    m = z.shape[0]
    ublk = 3 * D_ATTN // D_SGU
    return pl.pallas_call(
        functools.partial(_sgu_body, tm=tm),
        grid=(m // tm,),
        in_specs=[pl.BlockSpec((tm, D_SGU), lambda i: (i, ublk)),
                  pl.BlockSpec((tm, D_SGU), lambda i: (i, ublk + 1)),
                  pl.BlockSpec((1, D_SGU), lambda i: (0, 0)),
                  pl.BlockSpec((N_GROUPS, CHUNK, CHUNK), lambda i: (0, 0, 0)),
                  pl.BlockSpec((CHUNK, N_GROUPS), lambda i: (0, 0)),
                  pl.BlockSpec((1, D_SGU), lambda i: (0, 1))],
        out_specs=pl.BlockSpec((tm, D_SGU), lambda i: (i, 0)),
        out_shape=jax.ShapeDtypeStruct((m, D_SGU), BF16),
        compiler_params=_params("parallel"),
        name="sgu",
    )(z, z, sgu_norm_g.reshape(1, D_SGU), w_spatial, b_spatial_t, mix_g)


def _outproj_body(a_ref, g_ref, wa_ref, wg_ref, x_ref, o_ref):
    acc = jnp.dot(a_ref[...], wa_ref[...], preferred_element_type=F32)
    acc = acc + jnp.dot(g_ref[...], wg_ref[...], preferred_element_type=F32)
    o_ref[...] = x_ref[...] + acc


def _outproj(a, g, w, x, tm=1024, tn=512):
    m, ka = a.shape
    kg = g.shape[1]
    n = w.shape[1]
    return pl.pallas_call(
        _outproj_body,
        grid=(m // tm, n // tn),
        in_specs=[pl.BlockSpec((tm, ka), lambda i, j: (i, 0)),
                  pl.BlockSpec((tm, kg), lambda i, j: (i, 0)),
                  pl.BlockSpec((ka, tn), lambda i, j: (0, j)),
                  pl.BlockSpec((kg, tn), lambda i, j: (1, j)),
                  pl.BlockSpec((tm, tn), lambda i, j: (i, j))],
        out_specs=pl.BlockSpec((tm, tn), lambda i, j: (i, j)),
        out_shape=jax.ShapeDtypeStruct((m, n), F32),
        compiler_params=_params("parallel", "arbitrary"),
        name="outproj",
    )(a, g, w, w, x)


def _up_body(h_ref, wg_ref, wv_ref, cwg_ref, cwv_ref, cbg_ref, cbv_ref, o_ref, ext_ref, carry_ref,
             *, tm, nj, blocks_per_seq):
    i = pl.program_id(0)
    j = pl.program_id(1)
    seq_start = (i % blocks_per_seq) == 0
    h = h_ref[...]

    def conv_half(half, w_ref, cw_ref, cb_ref):
        up = jnp.dot(h, w_ref[...], preferred_element_type=F32)
        slot = half * nj + j
        prev = jnp.where(seq_start, 0.0, carry_ref[slot])
        carry_ref[slot] = up[tm - CARRY_ROWS:, :]
        ext_ref[half, 0:CARRY_ROWS, :] = prev
        ext_ref[half, CARRY_ROWS:, :] = up
        y = cb_ref[...] + cw_ref[0:1, :] * ext_ref[half, CARRY_ROWS - 2:CARRY_ROWS - 2 + tm, :]
        y = y + cw_ref[1:2, :] * ext_ref[half, CARRY_ROWS - 1:CARRY_ROWS - 1 + tm, :]
        return y + cw_ref[2:3, :] * up

    gate = conv_half(0, wg_ref, cwg_ref, cbg_ref)
    val = conv_half(1, wv_ref, cwv_ref, cbv_ref)
    o_ref[...] = (jax.nn.silu(gate) * val).astype(o_ref.dtype)


def _upproj(h, w, conv_w, conv_b, seq, tm=1024, tn=256):
    m, k = h.shape
    nj = D_FF // tn
    body = functools.partial(_up_body, tm=tm, nj=nj, blocks_per_seq=seq // tm)
    return pl.pallas_call(
        body,
        grid=(m // tm, nj),
        in_specs=[pl.BlockSpec((tm, k), lambda i, j: (i, 0)),
                  pl.BlockSpec((k, tn), lambda i, j: (0, j)),
                  pl.BlockSpec((k, tn), lambda i, j: (0, j + nj)),
                  pl.BlockSpec((CONV_WIDTH, tn), lambda i, j: (0, j)),
                  pl.BlockSpec((CONV_WIDTH, tn), lambda i, j: (0, j + nj)),
                  pl.BlockSpec((1, tn), lambda i, j: (0, j)),
                  pl.BlockSpec((1, tn), lambda i, j: (0, j + nj))],
        out_specs=pl.BlockSpec((tm, tn), lambda i, j: (i, j)),
        out_shape=jax.ShapeDtypeStruct((m, D_FF), BF16),
        scratch_shapes=[pltpu.VMEM((2, tm + CARRY_ROWS, tn), F32),
                        pltpu.VMEM((2 * nj, CARRY_ROWS, tn), F32)],
        compiler_params=_params("arbitrary", "arbitrary"),
        name="upproj_conv_gate",
    )(h, w, w, conv_w, conv_w, conv_b.reshape(1, -1), conv_b.reshape(1, -1))


def _down_body(a_ref, w_ref, x_ref, o_ref):
    o_ref[...] = x_ref[...] + jnp.dot(a_ref[...], w_ref[...], preferred_element_type=F32)


def _downproj(a, w, x, tm=512, tn=512):
    m, k = a.shape
    n = w.shape[1]
    return pl.pallas_call(
        _down_body,
        grid=(m // tm, n // tn),
        in_specs=[pl.BlockSpec((tm, k), lambda i, j: (i, 0)),
                  pl.BlockSpec((k, tn), lambda i, j: (0, j)),
                  pl.BlockSpec((tm, tn), lambda i, j: (i, j))],
        out_specs=pl.BlockSpec((tm, tn), lambda i, j: (i, j)),
        out_shape=jax.ShapeDtypeStruct((m, n), F32),
        compiler_params=_params("parallel", "arbitrary"),
        name="downproj",
    )(a, w, x)


def kernel(x, attn_norm_g, w_in, sgu_norm_g, w_spatial, b_spatial, mix_norm_g, w_out, ffn_norm_g,
           w_up, conv_w, conv_b, w_down, final_norm_g):
    b, s, d = x.shape
    depth = w_in.shape[0]
    xm = x.reshape(b * s, d)
    slopes = 2.0 ** (-8.0 * jnp.arange(1, N_HEADS + 1, dtype=F32) / N_HEADS)
    slopes = jnp.broadcast_to(slopes[:, None, None], (N_HEADS, 1, HEAD_DIM))
    for i in range(depth):
        mix_g = mix_norm_g[i].reshape(1, D_MODEL)
        h = _rmsnorm(xm, attn_norm_g[i], BF16)
        z = _inproj(h, w_in[i].astype(BF16))
        a = _attention(z.reshape(b, s, N_IN), slopes, mix_g).reshape(b * s, D_ATTN)
        g = _sgu(z, sgu_norm_g[i], w_spatial[i], b_spatial[i].T, mix_g)
        xm = _outproj(a, g, w_out[i].astype(BF16), xm)
        h = _rmsnorm(xm, ffn_norm_g[i], BF16)
        act = _upproj(h, w_up[i].astype(BF16), conv_w[i], conv_b[i], s)
        xm = _downproj(act, w_down[i].astype(BF16), xm)
    return _rmsnorm(xm, final_norm_g, F32).reshape(b, s, d)
```

```python
import functools
import math

import jax
import jax.numpy as jnp
from jax import lax
from jax.experimental import pallas as pl
from jax.experimental.pallas import tpu as pltpu

F32 = jnp.float32
BF16 = jnp.bfloat16

D_MODEL = 4096
HEAD_DIM = 128
D_ATTN = D_MODEL // 2
D_SGU = D_MODEL - D_ATTN
N_HEADS = D_ATTN // HEAD_DIM
N_GROUPS = D_SGU // HEAD_DIM
CHUNK = 128
BLOCK = 128
DILATIONS = (1, 4, 16)
N_IN = 3 * D_ATTN + 2 * D_SGU
D_FF = 256 * ((8 * D_MODEL // 3 + 255) // 256)
CONV_WIDTH = 3
EPS = 1e-6
LOG2E = math.log2(math.e)
VMEM_LIMIT = 56 * 1024 * 1024
CARRY_ROWS = 8
EPI_ROWS = 32
K_SPLIT = 16
ATT_GROUP = 4


def _params(*sem):
    return pltpu.CompilerParams(dimension_semantics=sem, vmem_limit_bytes=VMEM_LIMIT)


def _rmsnorm_body(x_ref, g_ref, o_ref):
    x = x_ref[...]
    y = x * lax.rsqrt(jnp.mean(x * x, axis=-1, keepdims=True) + EPS)
    o_ref[...] = (y * g_ref[...]).astype(o_ref.dtype)


def _rmsnorm(x, g, out_dtype, tm=256):
    m, d = x.shape
    return pl.pallas_call(
        _rmsnorm_body,
        grid=(m // tm,),
        in_specs=[pl.BlockSpec((tm, d), lambda i: (i, 0)),
                  pl.BlockSpec((1, d), lambda i: (0, 0))],
        out_specs=pl.BlockSpec((tm, d), lambda i: (i, 0)),
        out_shape=jax.ShapeDtypeStruct((m, d), out_dtype),
        compiler_params=_params("parallel"),
        name="rmsnorm",
    )(x, g.reshape(1, d))


def _inproj_body(h_ref, w_ref, o_ref):
    w = w_ref[...].astype(BF16)
    o_ref[...] = jnp.dot(h_ref[...], w, preferred_element_type=F32).astype(o_ref.dtype)


def _inproj(h, w, layer, tm=1024, tn=512):
    m, k = h.shape
    n = w.shape[2]
    return pl.pallas_call(
        _inproj_body,
        grid=(m // tm, n // tn),
        in_specs=[pl.BlockSpec((tm, k), lambda i, j: (i, 0)),
                  pl.BlockSpec((None, k, tn), lambda i, j: (layer, 0, j))],
        out_specs=pl.BlockSpec((tm, tn), lambda i, j: (i, j)),
        out_shape=jax.ShapeDtypeStruct((m, n), BF16),
        compiler_params=_params("parallel", "arbitrary"),
        name="inproj",
    )(h, w)


def _aligned(i, m):
    return i if isinstance(i, int) else pl.multiple_of(i, m)


def _attn_body(q_ref, k_ref, v_ref, slope_ref, g_ref, o_ref, xf, qd, kd, vd, bias2, ob, mb, lb, *, seq):
    qscale = HEAD_DIM ** -0.5 * LOG2E
    for src, dst, mul, first_c in ((q_ref, qd, qscale, 0), (k_ref, kd, None, 1), (v_ref, vd, None, 1)):
        xf[...] = src[...].astype(F32)
        if first_c == 0:
            dst[0] = (xf[...] * mul).astype(BF16)
        for c, d in enumerate(DILATIONS[1:]):
            sub = seq // d

            def deint(r, carry, dst=dst, slot=c + 1 - first_c, d=d, sub=sub, mul=mul):
                rows = pl.ds(pl.multiple_of(r * sub, sub), sub)
                x = xf[pl.ds(r, sub, stride=d), :]
                dst[slot, rows, :] = (x if mul is None else x * mul).astype(BF16)
                return carry

            lax.fori_loop(0, d, deint, 0)

    slope = slope_ref[...]
    qi = lax.broadcasted_iota(jnp.int32, (BLOCK, 2 * BLOCK), 0)
    kj = lax.broadcasted_iota(jnp.int32, (BLOCK, 2 * BLOCK), 1)
    dist = qi + BLOCK - kj
    valid = (dist >= 0) & (dist <= BLOCK)
    for c, d in enumerate(DILATIONS):
        bias2[c] = jnp.where(valid, (-LOG2E * slope[:, :1]) * (dist * d).astype(F32), -jnp.inf)
    ones = jnp.ones((2 * BLOCK, HEAD_DIM), BF16)

    def group(c, d, blocks):
        getk = (lambda rows: k_ref[rows, :]) if c == 0 else (lambda rows: kd[c - 1, rows, :])
        getv = (lambda rows: v_ref[rows, :]) if c == 0 else (lambda rows: vd[c - 1, rows, :])
        qs, ks, vs = [], [], []
        for gi, _, first in blocks:
            qrows = pl.ds(_aligned(gi * BLOCK, BLOCK), BLOCK)
            krows = qrows if first else pl.ds(_aligned(gi * BLOCK - BLOCK, BLOCK), 2 * BLOCK)
            qs.append(qd[c, qrows, :])
            ks.append(getk(krows))
            v = getv(krows)
            vs.append(jnp.concatenate([v, ones[:v.shape[0]]], axis=1))
        ss = [lax.dot_general(q, k, (((1,), (1,)), ((), ())), preferred_element_type=F32)
              for q, k in zip(qs, ks)]
        ps, ms = [], []
        for s, (_, _, first) in zip(ss, blocks):
            s = s + (bias2[c, :, BLOCK:] if first else bias2[c])
            mx = jnp.max(s, axis=-1, keepdims=True)
            ms.append(mx)
            ps.append(jnp.exp2(s - mx).astype(BF16))
        os = [jnp.dot(p, v, preferred_element_type=F32) for p, v in zip(ps, vs)]
        for o, mx, (_, q0, _) in zip(os, ms, blocks):
            rows = pl.ds(_aligned(q0, BLOCK), BLOCK) if d == 1 else pl.ds(q0, BLOCK, stride=d)
            ob[c, rows, :] = o[:, :HEAD_DIM]
            lb[c, rows, :] = o[:, HEAD_DIM:]
            mb[c, rows, :] = jnp.broadcast_to(mx, (BLOCK, HEAD_DIM))

    for c, d in enumerate(DILATIONS):
        nb = seq // d // BLOCK

        def blk(r, n, first, d=d, nb=nb):
            return (r * nb + n, r + d * BLOCK * n, first)

        if nb >= ATT_GROUP:
            def r_body(r, carry, c=c, d=d, nb=nb, blk=blk):
                group(c, d, [blk(r, n, n == 0) for n in range(ATT_GROUP)])

                def n_body(i, carry2):
                    group(c, d, [blk(r, i * ATT_GROUP + t, False) for t in range(ATT_GROUP)])
                    return carry2

                return lax.fori_loop(1, nb // ATT_GROUP, n_body, carry)

            if d == 1:
                r_body(0, 0)
            else:
                lax.fori_loop(0, d, r_body, 0)
        else:
            per = ATT_GROUP // nb

            def r_body(i, carry, c=c, d=d, nb=nb, blk=blk, per=per):
                group(c, d, [blk(i * per + t, n, n == 0) for t in range(per) for n in range(nb)])
                return carry

            lax.fori_loop(0, d // per, r_body, 0)

    g = g_ref[...]
    step = 256

    def fin(t, carry):
        rows = pl.ds(pl.multiple_of(t * step, step), step)
        m0, m1, m2 = mb[0, rows, :], mb[1, rows, :], mb[2, rows, :]
        mx = jnp.maximum(jnp.maximum(m0, m1), m2)
        e0, e1, e2 = jnp.exp2(m0 - mx), jnp.exp2(m1 - mx), jnp.exp2(m2 - mx)
        den = e0 * lb[0, rows, :] + e1 * lb[1, rows, :] + e2 * lb[2, rows, :]
        a = (e0 * ob[0, rows, :] + e1 * ob[1, rows, :] + e2 * ob[2, rows, :]) * (1.0 / den)
        y = a * lax.rsqrt(jnp.mean(a * a, axis=-1, keepdims=True) + EPS)
        o_ref[rows, :] = (y * g).astype(o_ref.dtype)
        return carry

    lax.fori_loop(0, seq // step, fin, 0)


def _attention(z3, slopes, mix_g):
    b, s, _ = z3.shape
    blk = lambda off: pl.BlockSpec((None, s, HEAD_DIM), lambda bi, h, off=off: (bi, 0, off + h))
    return pl.pallas_call(
        functools.partial(_attn_body, seq=s),
        grid=(b, N_HEADS),
        in_specs=[blk(0), blk(N_HEADS), blk(2 * N_HEADS),
                  pl.BlockSpec((None, 1, HEAD_DIM), lambda bi, h: (h, 0, 0)),
                  pl.BlockSpec((1, HEAD_DIM), lambda bi, h: (0, h))],
        out_specs=pl.BlockSpec((None, s, HEAD_DIM), lambda bi, h: (bi, 0, h)),
        out_shape=jax.ShapeDtypeStruct((b, s, D_ATTN), BF16),
        scratch_shapes=[pltpu.VMEM((s, HEAD_DIM), F32),
                        pltpu.VMEM((len(DILATIONS), s, HEAD_DIM), BF16)]
        + [pltpu.VMEM((len(DILATIONS) - 1, s, HEAD_DIM), BF16)] * 2
        + [pltpu.VMEM((len(DILATIONS), BLOCK, 2 * BLOCK), F32)]
        + [pltpu.VMEM((len(DILATIONS), s, HEAD_DIM), F32)] * 3,
        compiler_params=_params("parallel", "parallel"),
        name="dilated_attn",
    )(z3, z3, z3, slopes, mix_g)


def _gelu(x):
    return 0.5 * x * (1.0 + lax.erf(x * math.sqrt(0.5)))


def _sgu_body(u_ref, v_ref, ng_ref, w_ref, bt_ref, mg_ref, o_ref, *, tm):
    ti = lax.broadcasted_iota(jnp.int32, (CHUNK, CHUNK), 0)
    si = lax.broadcasted_iota(jnp.int32, (CHUNK, CHUNK), 1)
    causal = si <= ti
    for g in range(N_GROUPS):
        cols = slice(g * HEAD_DIM, (g + 1) * HEAD_DIM)
        w = jnp.where(causal, w_ref[g], 0.0).astype(BF16)
        bcol = bt_ref[:, g:g + 1]
        ng = ng_ref[:, cols]
        mg = mg_ref[:, cols]
        for ch in range(tm // CHUNK):
            rows = slice(ch * CHUNK, (ch + 1) * CHUNK)
            u = _gelu(u_ref[rows, cols].astype(F32))
            v = _gelu(v_ref[rows, cols].astype(F32))
            v = v * lax.rsqrt(jnp.mean(v * v, axis=-1, keepdims=True) + EPS) * ng
            mixed = jnp.dot(w, v.astype(BF16), preferred_element_type=F32) + bcol
            y = u * mixed
            y = y * lax.rsqrt(jnp.mean(y * y, axis=-1, keepdims=True) + EPS) * mg
            o_ref[rows, cols] = y.astype(o_ref.dtype)


def _sgu(z, sgu_norm_g, w_spatial, b_spatial_t, mix_g, tm=256):
    m = z.shape[0]
    ublk = 3 * D_ATTN // D_SGU
    return pl.pallas_call(
        functools.partial(_sgu_body, tm=tm),
        grid=(m // tm,),
        in_specs=[pl.BlockSpec((tm, D_SGU), lambda i: (i, ublk)),
                  pl.BlockSpec((tm, D_SGU), lambda i: (i, ublk + 1)),
                  pl.BlockSpec((1, D_SGU), lambda i: (0, 0)),
                  pl.BlockSpec((N_GROUPS, CHUNK, CHUNK), lambda i: (0, 0, 0)),
                  pl.BlockSpec((CHUNK, N_GROUPS), lambda i: (0, 0)),
                  pl.BlockSpec((1, D_SGU), lambda i: (0, 1))],
        out_specs=pl.BlockSpec((tm, D_SGU), lambda i: (i, 0)),
        out_shape=jax.ShapeDtypeStruct((m, D_SGU), BF16),
        compiler_params=_params("parallel"),
        name="sgu",
    )(z, z, sgu_norm_g.reshape(1, D_SGU), w_spatial, b_spatial_t, mix_g)


def _outproj_body(a_ref, g_ref, wa_ref, wg_ref, x_ref, o_ref):
    acc = jnp.dot(a_ref[...], wa_ref[...].astype(BF16), preferred_element_type=F32)
    acc = acc + jnp.dot(g_ref[...], wg_ref[...].astype(BF16), preferred_element_type=F32)
    o_ref[...] = x_ref[...] + acc


def _outproj(a, g, w, layer, x, tm=1024, tn=512):
    m, ka = a.shape
    kg = g.shape[1]
    assert ka == kg
    n = w.shape[2]
    return pl.pallas_call(
        _outproj_body,
        grid=(m // tm, n // tn),
        in_specs=[pl.BlockSpec((tm, ka), lambda i, j: (i, 0)),
                  pl.BlockSpec((tm, kg), lambda i, j: (i, 0)),
                  pl.BlockSpec((None, ka, tn), lambda i, j: (layer, 0, j)),
                  pl.BlockSpec((None, kg, tn), lambda i, j: (layer, 1, j)),
                  pl.BlockSpec((tm, tn), lambda i, j: (i, j))],
        out_specs=pl.BlockSpec((tm, tn), lambda i, j: (i, j)),
        out_shape=jax.ShapeDtypeStruct((m, n), F32),
        compiler_params=_params("parallel", "arbitrary"),
        name="outproj",
    )(a, g, w, w, x)


def _up_body(h_ref, wg_ref, wv_ref, cwg_ref, cwv_ref, cbg_ref, cbv_ref, o_ref, ext_a, ext_b, carry_ref,
             *, tm, nj, blocks_per_seq):
    t = pl.program_id(0)
    te = jnp.maximum(t - 1, 0)
    je = te % nj
    seq_start = ((te // nj) % blocks_per_seq) == 0

    @pl.when(t == 0)
    def _():
        ext_a[...] = jnp.zeros_like(ext_a)
        ext_b[...] = jnp.zeros_like(ext_b)
        carry_ref[...] = jnp.zeros_like(carry_ref)

    def stage(ext_w, ext_r):
        for half in range(2):
            slot = half * nj + je
            prev = jnp.where(seq_start, 0.0, carry_ref[slot])
            carry_ref[slot] = ext_r[half, tm:tm + CARRY_ROWS, :]
            ext_r[half, 0:CARRY_ROWS, :] = prev

        def conv(half, r0, cw_ref, cb_ref):
            base = CARRY_ROWS - (CONV_WIDTH - 1) + r0
            y = cb_ref[...] + cw_ref[0:1, :] * ext_r[half, base:base + EPI_ROWS, :]
            y = y + cw_ref[1:2, :] * ext_r[half, base + 1:base + 1 + EPI_ROWS, :]
            return y + cw_ref[2:3, :] * ext_r[half, base + 2:base + 2 + EPI_ROWS, :]

        kc = h_ref.shape[1] // K_SPLIT
        rows_per_piece = tm // (2 * K_SPLIT)
        piece = 0
        for half, w_ref in enumerate((wg_ref, wv_ref)):
            acc = None
            for p in range(K_SPLIT):
                ks = slice(p * kc, (p + 1) * kc)
                part = jnp.dot(h_ref[:, ks], w_ref[ks, :].astype(BF16), preferred_element_type=F32)
                acc = part if acc is None else acc + part
                for r0 in range(piece * rows_per_piece, (piece + 1) * rows_per_piece, EPI_ROWS):
                    gate = conv(0, r0, cwg_ref, cbg_ref)
                    val = conv(1, r0, cwv_ref, cbv_ref)
                    o_ref[r0:r0 + EPI_ROWS, :] = (jax.nn.silu(gate) * val).astype(o_ref.dtype)
                piece += 1
            ext_w[half, CARRY_ROWS:, :] = acc

    @pl.when(t % 2 == 0)
    def _():
        stage(ext_a, ext_b)

    @pl.when(t % 2 == 1)
    def _():
        stage(ext_b, ext_a)


def _upproj(h, w, conv_w, conv_b, layer, seq, tm=1024, tn=256):
    m, k = h.shape
    nj = D_FF // tn
    n_tiles = (m // tm) * nj
    body = functools.partial(_up_body, tm=tm, nj=nj, blocks_per_seq=seq // tm)
    mm = lambda t: jnp.minimum(t, n_tiles - 1)
    ep = lambda t: jnp.maximum(t - 1, 0)
    return pl.pallas_call(
        body,
        grid=(n_tiles + 1,),
        in_specs=[pl.BlockSpec((tm, k), lambda t: (mm(t) // nj, 0)),
                  pl.BlockSpec((None, k, tn), lambda t: (layer, 0, mm(t) % nj)),
                  pl.BlockSpec((None, k, tn), lambda t: (layer, 0, mm(t) % nj + nj)),
                  pl.BlockSpec((None, CONV_WIDTH, tn), lambda t: (layer, 0, ep(t) % nj)),
                  pl.BlockSpec((None, CONV_WIDTH, tn), lambda t: (layer, 0, ep(t) % nj + nj)),
                  pl.BlockSpec((None, 1, tn), lambda t: (layer, 0, ep(t) % nj)),
                  pl.BlockSpec((None, 1, tn), lambda t: (layer, 0, ep(t) % nj + nj))],
        out_specs=pl.BlockSpec((tm, tn), lambda t: (ep(t) // nj, ep(t) % nj)),
        out_shape=jax.ShapeDtypeStruct((m, D_FF), BF16),
        scratch_shapes=[pltpu.VMEM((2, tm + CARRY_ROWS, tn), F32),
                        pltpu.VMEM((2, tm + CARRY_ROWS, tn), F32),
                        pltpu.VMEM((2 * nj, CARRY_ROWS, tn), F32)],
        compiler_params=_params("arbitrary"),
        name="upproj_conv_gate",
    )(h, w, w, conv_w, conv_w, conv_b, conv_b)


def _down_body(a_ref, w_ref, x_ref, o_ref):
    o_ref[...] = x_ref[...] + jnp.dot(a_ref[...], w_ref[...], preferred_element_type=F32)


def _downproj(a, w, layer, x, tm=512, tn=512):
    m, k = a.shape
    n = w.shape[2]
    return pl.pallas_call(
        _down_body,
        grid=(m // tm, n // tn),
        in_specs=[pl.BlockSpec((tm, k), lambda i, j: (i, 0)),
                  pl.BlockSpec((None, k, tn), lambda i, j: (layer, 0, j)),
                  pl.BlockSpec((tm, tn), lambda i, j: (i, j))],
        out_specs=pl.BlockSpec((tm, tn), lambda i, j: (i, j)),
        out_shape=jax.ShapeDtypeStruct((m, n), F32),
        compiler_params=_params("parallel", "arbitrary"),
        name="downproj",
    )(a, w, x)


def kernel(x, attn_norm_g, w_in, sgu_norm_g, w_spatial, b_spatial, mix_norm_g, w_out, ffn_norm_g,
           w_up, conv_w, conv_b, w_down, final_norm_g):
    b, s, d = x.shape
    depth = w_in.shape[0]
    xm = x.reshape(b * s, d)
    slopes = 2.0 ** (-8.0 * jnp.arange(1, N_HEADS + 1, dtype=F32) / N_HEADS)
    slopes = jnp.broadcast_to(slopes[:, None, None], (N_HEADS, 1, HEAD_DIM))
    w_down_bf = w_down.astype(BF16)
    conv_b3 = conv_b.reshape(depth, 1, 2 * D_FF)
    for i in range(depth):
        mix_g = mix_norm_g[i].reshape(1, D_MODEL)
        h = _rmsnorm(xm, attn_norm_g[i], BF16)
        z = _inproj(h, w_in, i)
        a = _attention(z.reshape(b, s, N_IN), slopes, mix_g).reshape(b * s, D_ATTN)
        g = _sgu(z, sgu_norm_g[i], w_spatial[i], b_spatial[i].T, mix_g)
        xm = _outproj(a, g, w_out, i, xm)
        h = _rmsnorm(xm, ffn_norm_g[i], BF16)
        act = _upproj(h, w_up, conv_w, conv_b3, i, s)
        xm = _downproj(act, w_down_bf, i, xm)
    return _rmsnorm(xm, final_norm_g, F32).reshape(b, s, d)
```

```python
import functools
import math

import jax
import jax.numpy as jnp
from jax import lax
from jax.experimental import pallas as pl
from jax.experimental.pallas import tpu as pltpu

F32 = jnp.float32
BF16 = jnp.bfloat16

D_MODEL = 4096
HEAD_DIM = 128
D_ATTN = D_MODEL // 2
D_SGU = D_MODEL - D_ATTN
N_HEADS = D_ATTN // HEAD_DIM
N_GROUPS = D_SGU // HEAD_DIM
CHUNK = 128
BLOCK = 128
DILATIONS = (1, 4, 16)
N_IN = 3 * D_ATTN + 2 * D_SGU
D_FF = 256 * ((8 * D_MODEL // 3 + 255) // 256)
CONV_WIDTH = 3
EPS = 1e-6
LOG2E = math.log2(math.e)
VMEM_LIMIT = 56 * 1024 * 1024
CARRY_ROWS = 8
EPI_ROWS = 32
K_SPLIT = 16
SGU_BATCH = 4
ATT_GROUP = 8


def _params(*sem):
    return pltpu.CompilerParams(dimension_semantics=sem, vmem_limit_bytes=VMEM_LIMIT)


def _rmsnorm_body(x_ref, g_ref, o_ref):
    x = x_ref[...]
    y = x * lax.rsqrt(jnp.mean(x * x, axis=-1, keepdims=True) + EPS)
    o_ref[...] = (y * g_ref[...]).astype(o_ref.dtype)


def _rmsnorm(x, g, out_dtype, tm=256):
    m, d = x.shape
    return pl.pallas_call(
        _rmsnorm_body,
        grid=(m // tm,),
        in_specs=[pl.BlockSpec((tm, d), lambda i: (i, 0)),
                  pl.BlockSpec((1, d), lambda i: (0, 0))],
        out_specs=pl.BlockSpec((tm, d), lambda i: (i, 0)),
        out_shape=jax.ShapeDtypeStruct((m, d), out_dtype),
        compiler_params=_params("parallel"),
        name="rmsnorm",
    )(x, g.reshape(1, d))


def _inproj_body(h_ref, w_ref, o_ref):
    w = w_ref[...].astype(BF16)
    o_ref[...] = jnp.dot(h_ref[...], w, preferred_element_type=F32).astype(o_ref.dtype)


def _inproj(h, w, layer, tm=1024, tn=512):
    m, k = h.shape
    n = w.shape[2]
    return pl.pallas_call(
        _inproj_body,
        grid=(m // tm, n // tn),
        in_specs=[pl.BlockSpec((tm, k), lambda i, j: (i, 0)),
                  pl.BlockSpec((None, k, tn), lambda i, j: (layer, 0, j))],
        out_specs=pl.BlockSpec((tm, tn), lambda i, j: (i, j)),
        out_shape=jax.ShapeDtypeStruct((m, n), BF16),
        compiler_params=_params("parallel", "arbitrary"),
        name="inproj",
    )(h, w)


def _aligned(i, m):
    return i if isinstance(i, int) else pl.multiple_of(i, m)


def _attn_body(q_ref, k_ref, v_ref, slope_ref, g_ref, o_ref, xf, x4f, qd, kd, vd, bias2, ob, mb, lb,
               *, seq):
    qscale = HEAD_DIM ** -0.5 * LOG2E
    d1, d2 = DILATIONS[1], DILATIONS[2]
    assert d2 == d1 * d1 and DILATIONS[0] == 1
    sub1, sub2 = seq // d1, seq // d2
    for src, dst, mul, first_c in ((q_ref, qd, qscale, 0), (k_ref, kd, None, 1), (v_ref, vd, None, 1)):
        if first_c == 0:
            xf[...] = src[...].astype(F32) * mul
            dst[0] = xf[...].astype(BF16)
        else:
            xf[...] = src[...].astype(F32)

        def deint1(r, carry, dst=dst, slot=1 - first_c):
            rows = pl.ds(pl.multiple_of(r * sub1, sub1), sub1)
            x = xf[pl.ds(r, sub1, stride=d1), :]
            x4f[rows, :] = x
            dst[slot, rows, :] = x.astype(BF16)
            return carry

        lax.fori_loop(0, d1, deint1, 0)

        def deint2(r, carry, dst=dst, slot=2 - first_c):
            rows = pl.ds(pl.multiple_of(r * sub2, sub2), sub2)
            x = x4f[pl.ds((r % d1) * sub1 + r // d1, sub2, stride=d1), :]
            dst[slot, rows, :] = x.astype(BF16)
            return carry

        lax.fori_loop(0, d2, deint2, 0)

    slope = slope_ref[...]
    qi = lax.broadcasted_iota(jnp.int32, (BLOCK, 2 * BLOCK), 0)
    kj = lax.broadcasted_iota(jnp.int32, (BLOCK, 2 * BLOCK), 1)
    dist = qi + BLOCK - kj
    valid = (dist >= 0) & (dist <= BLOCK)
    for c, d in enumerate(DILATIONS):
        bias2[c] = jnp.where(valid, (-LOG2E * slope[:, :1]) * (dist * d).astype(F32), -jnp.inf)
    ones = jnp.ones((2 * BLOCK, HEAD_DIM), BF16)

    def group(c, d, blocks):
        getk = (lambda rows: k_ref[rows, :]) if c == 0 else (lambda rows: kd[c - 1, rows, :])
        getv = (lambda rows: v_ref[rows, :]) if c == 0 else (lambda rows: vd[c - 1, rows, :])
        qs, ks, vs = [], [], []
        for gi, _, first in blocks:
            qrows = pl.ds(_aligned(gi * BLOCK, BLOCK), BLOCK)
            krows = qrows if first else pl.ds(_aligned(gi * BLOCK - BLOCK, BLOCK), 2 * BLOCK)
            qs.append(qd[c, qrows, :])
            ks.append(getk(krows))
            v = getv(krows)
            vs.append(jnp.concatenate([v, ones[:v.shape[0]]], axis=1))
        ss = [lax.dot_general(q, k, (((1,), (1,)), ((), ())), preferred_element_type=F32)
              for q, k in zip(qs, ks)]
        ps, ms = [], []
        for s, (_, _, first) in zip(ss, blocks):
            s = s + (bias2[c, :, BLOCK:] if first else bias2[c])
            mx = jnp.max(s, axis=-1, keepdims=True)
            ms.append(mx)
            ps.append(jnp.exp2(s - mx).astype(BF16))
        os = [jnp.dot(p, v, preferred_element_type=F32) for p, v in zip(ps, vs)]
        for o, mx, (_, rows, _) in zip(os, ms, blocks):
            ob[c, rows, :] = o[:, :HEAD_DIM]
            lb[c, rows, :] = o[:, HEAD_DIM:]
            mb[c, rows, :] = jnp.broadcast_to(mx, (BLOCK, HEAD_DIM))

    for c, d in enumerate(DILATIONS):
        nb = seq // d // BLOCK

        def blk(r, n, first, d=d, nb=nb):
            gi = r * nb + n
            if d <= d1:
                return (gi, pl.ds(_aligned(gi * BLOCK, BLOCK), BLOCK), first)
            q = d // d1
            start = (r % d1) * sub1 + q * BLOCK * n + r // d1
            return (gi, pl.ds(start, BLOCK, stride=q), first)

        if nb >= ATT_GROUP:
            def r_body(r, carry, c=c, d=d, nb=nb, blk=blk):
                group(c, d, [blk(r, n, n == 0) for n in range(ATT_GROUP)])

                def n_body(i, carry2):
                    group(c, d, [blk(r, i * ATT_GROUP + t, False) for t in range(ATT_GROUP)])
                    return carry2

                return lax.fori_loop(1, nb // ATT_GROUP, n_body, carry)

            if d == 1:
                r_body(0, 0)
            else:
                lax.fori_loop(0, d, r_body, 0)
        else:
            per = ATT_GROUP // nb

            def r_body(i, carry, c=c, d=d, nb=nb, blk=blk, per=per):
                group(c, d, [blk(i * per + t, n, n == 0) for t in range(per) for n in range(nb)])
                return carry

            lax.fori_loop(0, d // per, r_body, 0)

    g = g_ref[...]
    step, piece = 256, 64

    def fin(t, carry):
        r = (t * step) // sub1
        m0 = (t * step) % sub1
        for p0 in range(0, step, piece):
            rows = pl.ds(pl.multiple_of(t * step + p0, piece), piece)
            toks = pl.ds(r + d1 * (m0 + p0), piece, stride=d1)
            ma, mb1, mb2 = mb[0, toks, :], mb[1, rows, :], mb[2, rows, :]
            mx = jnp.maximum(jnp.maximum(ma, mb1), mb2)
            e0, e1, e2 = jnp.exp2(ma - mx), jnp.exp2(mb1 - mx), jnp.exp2(mb2 - mx)
            den = e0 * lb[0, toks, :] + e1 * lb[1, rows, :] + e2 * lb[2, rows, :]
            a = (e0 * ob[0, toks, :] + e1 * ob[1, rows, :] + e2 * ob[2, rows, :]) * (1.0 / den)
            y = a * lax.rsqrt(jnp.mean(a * a, axis=-1, keepdims=True) + EPS)
            xf[toks, :] = y * g
        return carry

    lax.fori_loop(0, seq // step, fin, 0)
    o_ref[...] = xf[...].astype(o_ref.dtype)


def _attention(z3, slopes, mix_g):
    b, s, _ = z3.shape
    blk = lambda off: pl.BlockSpec((None, s, HEAD_DIM), lambda bi, h, off=off: (bi, 0, off + h))
    return pl.pallas_call(
        functools.partial(_attn_body, seq=s),
        grid=(b, N_HEADS),
        in_specs=[blk(0), blk(N_HEADS), blk(2 * N_HEADS),
                  pl.BlockSpec((None, 1, HEAD_DIM), lambda bi, h: (h, 0, 0)),
                  pl.BlockSpec((1, HEAD_DIM), lambda bi, h: (0, h))],
        out_specs=pl.BlockSpec((None, s, HEAD_DIM), lambda bi, h: (bi, 0, h)),
        out_shape=jax.ShapeDtypeStruct((b, s, D_ATTN), BF16),
        scratch_shapes=[pltpu.VMEM((s, HEAD_DIM), F32)] * 2
        + [pltpu.VMEM((len(DILATIONS), s, HEAD_DIM), BF16)]
        + [pltpu.VMEM((len(DILATIONS) - 1, s, HEAD_DIM), BF16)] * 2
        + [pltpu.VMEM((len(DILATIONS), BLOCK, 2 * BLOCK), F32)]
        + [pltpu.VMEM((len(DILATIONS), s, HEAD_DIM), F32)] * 3,
        compiler_params=_params("parallel", "parallel"),
        name="dilated_attn",
    )(z3, z3, z3, slopes, mix_g)


def _gelu(x):
    return 0.5 * x * (1.0 + lax.erf(x * math.sqrt(0.5)))


def _sgu_body(u_ref, v_ref, ng_ref, w_ref, bt_ref, mg_ref, o_ref, *, tm):
    ti = lax.broadcasted_iota(jnp.int32, (CHUNK, CHUNK), 0)
    si = lax.broadcasted_iota(jnp.int32, (CHUNK, CHUNK), 1)
    causal = si <= ti
    n_ch = tm // CHUNK
    for g0 in range(0, N_GROUPS, SGU_BATCH):
        groups = range(g0, g0 + SGU_BATCH)
        vcat = []
        for g in groups:
            cols = slice(g * HEAD_DIM, (g + 1) * HEAD_DIM)
            ng = ng_ref[:, cols]
            vs = []
            for ch in range(n_ch):
                v = _gelu(v_ref[ch * CHUNK:(ch + 1) * CHUNK, cols].astype(F32))
                v = v * lax.rsqrt(jnp.mean(v * v, axis=-1, keepdims=True) + EPS) * ng
                vs.append(v.astype(BF16))
            vcat.append(jnp.concatenate(vs, axis=1))
        mixed = [jnp.dot(jnp.where(causal, w_ref[g], 0.0).astype(BF16), vc, preferred_element_type=F32)
                 for g, vc in zip(groups, vcat)]
        for g, mx in zip(groups, mixed):
            cols = slice(g * HEAD_DIM, (g + 1) * HEAD_DIM)
            bcol = bt_ref[:, g:g + 1]
            mg = mg_ref[:, cols]
            for ch in range(n_ch):
                rows = slice(ch * CHUNK, (ch + 1) * CHUNK)
                u = _gelu(u_ref[rows, cols].astype(F32))
                y = u * (mx[:, ch * HEAD_DIM:(ch + 1) * HEAD_DIM] + bcol)
                y = y * lax.rsqrt(jnp.mean(y * y, axis=-1, keepdims=True) + EPS) * mg
                o_ref[rows, cols] = y.astype(o_ref.dtype)


def _sgu(z, sgu_norm_g, w_spatial, b_spatial_t, mix_g, tm=256):
    m = z.shape[0]
    ublk = 3 * D_ATTN // D_SGU
    return pl.pallas_call(
        functools.partial(_sgu_body, tm=tm),
        grid=(m // tm,),
        in_specs=[pl.BlockSpec((tm, D_SGU), lambda i: (i, ublk)),
                  pl.BlockSpec((tm, D_SGU), lambda i: (i, ublk + 1)),
                  pl.BlockSpec((1, D_SGU), lambda i: (0, 0)),
                  pl.BlockSpec((N_GROUPS, CHUNK, CHUNK), lambda i: (0, 0, 0)),
                  pl.BlockSpec((CHUNK, N_GROUPS), lambda i: (0, 0)),
                  pl.BlockSpec((1, D_SGU), lambda i: (0, 1))],
        out_specs=pl.BlockSpec((tm, D_SGU), lambda i: (i, 0)),
        out_shape=jax.ShapeDtypeStruct((m, D_SGU), BF16),
        compiler_params=_params("parallel"),
        name="sgu",
    )(z, z, sgu_norm_g.reshape(1, D_SGU), w_spatial, b_spatial_t, mix_g)


def _outproj_body(a_ref, g_ref, wa_ref, wg_ref, x_ref, o_ref):
    acc = jnp.dot(a_ref[...], wa_ref[...].astype(BF16), preferred_element_type=F32)
    acc = acc + jnp.dot(g_ref[...], wg_ref[...].astype(BF16), preferred_element_type=F32)
    o_ref[...] = x_ref[...] + acc


def _outproj(a, g, w, layer, x, tm=1024, tn=512):
    m, ka = a.shape
    kg = g.shape[1]
    assert ka == kg
    n = w.shape[2]
    return pl.pallas_call(
        _outproj_body,
        grid=(m // tm, n // tn),
        in_specs=[pl.BlockSpec((tm, ka), lambda i, j: (i, 0)),
                  pl.BlockSpec((tm, kg), lambda i, j: (i, 0)),
                  pl.BlockSpec((None, ka, tn), lambda i, j: (layer, 0, j)),
                  pl.BlockSpec((None, kg, tn), lambda i, j: (layer, 1, j)),
                  pl.BlockSpec((tm, tn), lambda i, j: (i, j))],
        out_specs=pl.BlockSpec((tm, tn), lambda i, j: (i, j)),
        out_shape=jax.ShapeDtypeStruct((m, n), F32),
        compiler_params=_params("parallel", "arbitrary"),
        name="outproj",
    )(a, g, w, w, x)


def _up_body(h_ref, wg_ref, wv_ref, cwg_ref, cwv_ref, cbg_ref, cbv_ref, o_ref, ext_a, ext_b, carry_ref,
             *, tm, nj, blocks_per_seq):
    t = pl.program_id(0)
    te = jnp.maximum(t - 1, 0)
    je = te % nj
    seq_start = ((te // nj) % blocks_per_seq) == 0

    @pl.when(t == 0)
    def _():
        ext_a[...] = jnp.zeros_like(ext_a)
        ext_b[...] = jnp.zeros_like(ext_b)
        carry_ref[...] = jnp.zeros_like(carry_ref)

    def stage(ext_w, ext_r):
        for half in range(2):
            slot = half * nj + je
            prev = jnp.where(seq_start, 0.0, carry_ref[slot])
            carry_ref[slot] = ext_r[half, tm:tm + CARRY_ROWS, :]
            ext_r[half, 0:CARRY_ROWS, :] = prev

        def conv(half, r0, cw_ref, cb_ref):
            base = CARRY_ROWS - (CONV_WIDTH - 1) + r0
            y = cb_ref[...] + cw_ref[0:1, :] * ext_r[half, base:base + EPI_ROWS, :]
            y = y + cw_ref[1:2, :] * ext_r[half, base + 1:base + 1 + EPI_ROWS, :]
            return y + cw_ref[2:3, :] * ext_r[half, base + 2:base + 2 + EPI_ROWS, :]

        kc = h_ref.shape[1] // K_SPLIT
        rows_per_piece = tm // (2 * K_SPLIT)
        piece = 0
        for half, w_ref in enumerate((wg_ref, wv_ref)):
            acc = None
            for p in range(K_SPLIT):
                ks = slice(p * kc, (p + 1) * kc)
                part = jnp.dot(h_ref[:, ks], w_ref[ks, :].astype(BF16), preferred_element_type=F32)
                acc = part if acc is None else acc + part
                for r0 in range(piece * rows_per_piece, (piece + 1) * rows_per_piece, EPI_ROWS):
                    gate = conv(0, r0, cwg_ref, cbg_ref)
                    val = conv(1, r0, cwv_ref, cbv_ref)
                    o_ref[r0:r0 + EPI_ROWS, :] = (jax.nn.silu(gate) * val).astype(o_ref.dtype)
                piece += 1
            ext_w[half, CARRY_ROWS:, :] = acc

    @pl.when(t % 2 == 0)
    def _():
        stage(ext_a, ext_b)

    @pl.when(t % 2 == 1)
    def _():
        stage(ext_b, ext_a)


def _upproj(h, w, conv_w, conv_b, layer, seq, tm=1024, tn=256):
    m, k = h.shape
    nj = D_FF // tn
    n_tiles = (m // tm) * nj
    body = functools.partial(_up_body, tm=tm, nj=nj, blocks_per_seq=seq // tm)
    mm = lambda t: jnp.minimum(t, n_tiles - 1)
    ep = lambda t: jnp.maximum(t - 1, 0)
    return pl.pallas_call(
        body,
        grid=(n_tiles + 1,),
        in_specs=[pl.BlockSpec((tm, k), lambda t: (mm(t) // nj, 0)),
                  pl.BlockSpec((None, k, tn), lambda t: (layer, 0, mm(t) % nj)),
                  pl.BlockSpec((None, k, tn), lambda t: (layer, 0, mm(t) % nj + nj)),
                  pl.BlockSpec((None, CONV_WIDTH, tn), lambda t: (layer, 0, ep(t) % nj)),
                  pl.BlockSpec((None, CONV_WIDTH, tn), lambda t: (layer, 0, ep(t) % nj + nj)),
                  pl.BlockSpec((None, 1, tn), lambda t: (layer, 0, ep(t) % nj)),
                  pl.BlockSpec((None, 1, tn), lambda t: (layer, 0, ep(t) % nj + nj))],
        out_specs=pl.BlockSpec((tm, tn), lambda t: (ep(t) // nj, ep(t) % nj)),
        out_shape=jax.ShapeDtypeStruct((m, D_FF), BF16),
        scratch_shapes=[pltpu.VMEM((2, tm + CARRY_ROWS, tn), F32),
                        pltpu.VMEM((2, tm + CARRY_ROWS, tn), F32),
                        pltpu.VMEM((2 * nj, CARRY_ROWS, tn), F32)],
        compiler_params=_params("arbitrary"),
        name="upproj_conv_gate",
    )(h, w, w, conv_w, conv_w, conv_b, conv_b)


def _down_body(a_ref, w_ref, x_ref, o_ref):
    o_ref[...] = x_ref[...] + jnp.dot(a_ref[...], w_ref[...], preferred_element_type=F32)


def _downproj(a, w, layer, x, tm=512, tn=512):
    m, k = a.shape
    n = w.shape[2]
    return pl.pallas_call(
        _down_body,
        grid=(m // tm, n // tn),
        in_specs=[pl.BlockSpec((tm, k), lambda i, j: (i, 0)),
                  pl.BlockSpec((None, k, tn), lambda i, j: (layer, 0, j)),
                  pl.BlockSpec((tm, tn), lambda i, j: (i, j))],
        out_specs=pl.BlockSpec((tm, tn), lambda i, j: (i, j)),
        out_shape=jax.ShapeDtypeStruct((m, n), F32),
        compiler_params=_params("parallel", "arbitrary"),
        name="downproj",
    )(a, w, x)


def kernel(x, attn_norm_g, w_in, sgu_norm_g, w_spatial, b_spatial, mix_norm_g, w_out, ffn_norm_g,
           w_up, conv_w, conv_b, w_down, final_norm_g):
    b, s, d = x.shape
    depth = w_in.shape[0]
    xm = x.reshape(b * s, d)
    slopes = 2.0 ** (-8.0 * jnp.arange(1, N_HEADS + 1, dtype=F32) / N_HEADS)
    slopes = jnp.broadcast_to(slopes[:, None, None], (N_HEADS, 1, HEAD_DIM))
    w_down_bf = w_down.astype(BF16)
    conv_b3 = conv_b.reshape(depth, 1, 2 * D_FF)
    for i in range(depth):
        mix_g = mix_norm_g[i].reshape(1, D_MODEL)
        h = _rmsnorm(xm, attn_norm_g[i], BF16)
        z = _inproj(h, w_in, i)
        a = _attention(z.reshape(b, s, N_IN), slopes, mix_g).reshape(b * s, D_ATTN)
        g = _sgu(z, sgu_norm_g[i], w_spatial[i], b_spatial[i].T, mix_g)
        xm = _outproj(a, g, w_out, i, xm)
        h = _rmsnorm(xm, ffn_norm_g[i], BF16)
        act = _upproj(h, w_up, conv_w, conv_b3, i, s)
        xm = _downproj(act, w_down_bf, i, xm)
    return _rmsnorm(xm, final_norm_g, F32).reshape(b, s, d)
```

```python
import functools
import math

import jax
import jax.numpy as jnp
from jax import lax
from jax.experimental import pallas as pl
from jax.experimental.pallas import tpu as pltpu

F32 = jnp.float32
BF16 = jnp.bfloat16

D_MODEL = 4096
HEAD_DIM = 128
D_ATTN = D_MODEL // 2
D_SGU = D_MODEL - D_ATTN
N_HEADS = D_ATTN // HEAD_DIM
N_GROUPS = D_SGU // HEAD_DIM
CHUNK = 128
BLOCK = 128
DILATIONS = (1, 4, 16)
N_IN = 3 * D_ATTN + 2 * D_SGU
D_FF = 256 * ((8 * D_MODEL // 3 + 255) // 256)
CONV_WIDTH = 3
EPS = 1e-6
LOG2E = math.log2(math.e)
VMEM_LIMIT = 56 * 1024 * 1024
CARRY_ROWS = 8
EPI_ROWS = 32
K_SPLIT = 16
SGU_BATCH = 4
ATT_GROUP = 16


def _params(*sem):
    return pltpu.CompilerParams(dimension_semantics=sem, vmem_limit_bytes=VMEM_LIMIT)


def _rmsnorm_body(x_ref, g_ref, o_ref):
    x = x_ref[...]
    y = x * lax.rsqrt(jnp.mean(x * x, axis=-1, keepdims=True) + EPS)
    o_ref[...] = (y * g_ref[...]).astype(o_ref.dtype)


def _rmsnorm(x, g, out_dtype, tm=256):
    m, d = x.shape
    return pl.pallas_call(
        _rmsnorm_body,
        grid=(m // tm,),
        in_specs=[pl.BlockSpec((tm, d), lambda i: (i, 0)),
                  pl.BlockSpec((1, d), lambda i: (0, 0))],
        out_specs=pl.BlockSpec((tm, d), lambda i: (i, 0)),
        out_shape=jax.ShapeDtypeStruct((m, d), out_dtype),
        compiler_params=_params("parallel"),
        name="rmsnorm",
    )(x, g.reshape(1, d))


def _inproj_body(h_ref, w_ref, o_ref):
    w = w_ref[...].astype(BF16)
    o_ref[...] = jnp.dot(h_ref[...], w, preferred_element_type=F32).astype(o_ref.dtype)


def _inproj(h, w, layer, tm=1024, tn=512):
    m, k = h.shape
    n = w.shape[2]
    return pl.pallas_call(
        _inproj_body,
        grid=(m // tm, n // tn),
        in_specs=[pl.BlockSpec((tm, k), lambda i, j: (i, 0)),
                  pl.BlockSpec((None, k, tn), lambda i, j: (layer, 0, j))],
        out_specs=pl.BlockSpec((tm, tn), lambda i, j: (i, j)),
        out_shape=jax.ShapeDtypeStruct((m, n), BF16),
        compiler_params=_params("parallel", "arbitrary"),
        name="inproj",
    )(h, w)


def _aligned(i, m):
    return i if isinstance(i, int) else pl.multiple_of(i, m)


def _attn_body(q_ref, k_ref, v_ref, slope_ref, g_ref, o_ref, xf, x4f, qd, kd, vd, bias2, ob, mb, lb,
               *, seq):
    qscale = HEAD_DIM ** -0.5 * LOG2E
    d1, d2 = DILATIONS[1], DILATIONS[2]
    assert d2 == d1 * d1 and DILATIONS[0] == 1
    sub1, sub2 = seq // d1, seq // d2
    for src, dst, mul, first_c in ((q_ref, qd, qscale, 0), (k_ref, kd, None, 1), (v_ref, vd, None, 1)):
        if first_c == 0:
            xf[...] = src[...].astype(F32) * mul
            dst[0] = xf[...].astype(BF16)
        else:
            xf[...] = src[...].astype(F32)

        def deint1(r, carry, dst=dst, slot=1 - first_c):
            rows = pl.ds(pl.multiple_of(r * sub1, sub1), sub1)
            x = xf[pl.ds(r, sub1, stride=d1), :]
            x4f[rows, :] = x
            dst[slot, rows, :] = x.astype(BF16)
            return carry

        lax.fori_loop(0, d1, deint1, 0)

        def deint2(r, carry, dst=dst, slot=2 - first_c):
            rows = pl.ds(pl.multiple_of(r * sub2, sub2), sub2)
            x = x4f[pl.ds((r % d1) * sub1 + r // d1, sub2, stride=d1), :]
            dst[slot, rows, :] = x.astype(BF16)
            return carry

        lax.fori_loop(0, d2, deint2, 0)

    slope = slope_ref[...]
    qi = lax.broadcasted_iota(jnp.int32, (BLOCK, 2 * BLOCK), 0)
    kj = lax.broadcasted_iota(jnp.int32, (BLOCK, 2 * BLOCK), 1)
    dist = qi + BLOCK - kj
    valid = (dist >= 0) & (dist <= BLOCK)
    for c, d in enumerate(DILATIONS):
        bias2[c] = jnp.where(valid, (-LOG2E * slope[:, :1]) * (dist * d).astype(F32), -jnp.inf)
    ones = jnp.ones((2 * BLOCK, HEAD_DIM), BF16)

    def group(c, d, blocks):
        getk = (lambda rows: k_ref[rows, :]) if c == 0 else (lambda rows: kd[c - 1, rows, :])
        getv = (lambda rows: v_ref[rows, :]) if c == 0 else (lambda rows: vd[c - 1, rows, :])
        qs, ks, vs = [], [], []
        for gi, _, first in blocks:
            qrows = pl.ds(_aligned(gi * BLOCK, BLOCK), BLOCK)
            krows = qrows if first else pl.ds(_aligned(gi * BLOCK - BLOCK, BLOCK), 2 * BLOCK)
            qs.append(qd[c, qrows, :])
            ks.append(getk(krows))
            v = getv(krows)
            vs.append(jnp.concatenate([v, ones[:v.shape[0]]], axis=1))
        ss = [lax.dot_general(q, k, (((1,), (1,)), ((), ())), preferred_element_type=F32)
              for q, k in zip(qs, ks)]
        ps, ms = [], []
        for s, (_, _, first) in zip(ss, blocks):
            s = s + (bias2[c, :, BLOCK:] if first else bias2[c])
            mx = jnp.max(s, axis=-1, keepdims=True)
            ms.append(mx)
            ps.append(jnp.exp2(s - mx).astype(BF16))
        os = [jnp.dot(p, v, preferred_element_type=F32) for p, v in zip(ps, vs)]
        for o, mx, (_, rows, _) in zip(os, ms, blocks):
            ob[c, rows, :] = o[:, :HEAD_DIM]
            lb[c, rows, :] = o[:, HEAD_DIM:]
            mb[c, rows, :] = jnp.broadcast_to(mx, (BLOCK, HEAD_DIM))

    for c, d in enumerate(DILATIONS):
        nb = seq // d // BLOCK

        def blk(r, n, first, d=d, nb=nb):
            gi = r * nb + n
            if d <= d1:
                return (gi, pl.ds(_aligned(gi * BLOCK, BLOCK), BLOCK), first)
            q = d // d1
            start = (r % d1) * sub1 + q * BLOCK * n + r // d1
            return (gi, pl.ds(start, BLOCK, stride=q), first)

        if nb >= ATT_GROUP:
            def r_body(r, carry, c=c, d=d, nb=nb, blk=blk):
                group(c, d, [blk(r, n, n == 0) for n in range(ATT_GROUP)])

                def n_body(i, carry2):
                    group(c, d, [blk(r, i * ATT_GROUP + t, False) for t in range(ATT_GROUP)])
                    return carry2

                return lax.fori_loop(1, nb // ATT_GROUP, n_body, carry)

            if d == 1:
                r_body(0, 0)
            else:
                lax.fori_loop(0, d, r_body, 0)
        else:
            per = ATT_GROUP // nb

            def r_body(i, carry, c=c, d=d, nb=nb, blk=blk, per=per):
                group(c, d, [blk(i * per + t, n, n == 0) for t in range(per) for n in range(nb)])
                return carry

            lax.fori_loop(0, d // per, r_body, 0)

    g = g_ref[...]
    step, piece = 512, 64

    def fin(t, carry):
        r = (t * step) // sub1
        m0 = (t * step) % sub1
        for p0 in range(0, step, piece):
            rows = pl.ds(pl.multiple_of(t * step + p0, piece), piece)
            toks = pl.ds(r + d1 * (m0 + p0), piece, stride=d1)
            ma, mb1, mb2 = mb[0, toks, :], mb[1, rows, :], mb[2, rows, :]
            mx = jnp.maximum(jnp.maximum(ma, mb1), mb2)
            e0, e1, e2 = jnp.exp2(ma - mx), jnp.exp2(mb1 - mx), jnp.exp2(mb2 - mx)
            den = e0 * lb[0, toks, :] + e1 * lb[1, rows, :] + e2 * lb[2, rows, :]
            num = e0 * ob[0, toks, :] + e1 * ob[1, rows, :] + e2 * ob[2, rows, :]
            msq = jnp.mean(num * num, axis=-1, keepdims=True)
            xf[toks, :] = num * lax.rsqrt(msq + EPS * (den * den)) * g
        return carry

    lax.fori_loop(0, seq // step, fin, 0)
    o_ref[...] = xf[...].astype(o_ref.dtype)


def _attention(z3, slopes, mix_g):
    b, s, _ = z3.shape
    blk = lambda off: pl.BlockSpec((None, s, HEAD_DIM), lambda bi, h, off=off: (bi, 0, off + h))
    return pl.pallas_call(
        functools.partial(_attn_body, seq=s),
        grid=(b, N_HEADS),
        in_specs=[blk(0), blk(N_HEADS), blk(2 * N_HEADS),
                  pl.BlockSpec((None, 1, HEAD_DIM), lambda bi, h: (h, 0, 0)),
                  pl.BlockSpec((1, HEAD_DIM), lambda bi, h: (0, h))],
        out_specs=pl.BlockSpec((None, s, HEAD_DIM), lambda bi, h: (bi, 0, h)),
        out_shape=jax.ShapeDtypeStruct((b, s, D_ATTN), BF16),
        scratch_shapes=[pltpu.VMEM((s, HEAD_DIM), F32)] * 2
        + [pltpu.VMEM((len(DILATIONS), s, HEAD_DIM), BF16)]
        + [pltpu.VMEM((len(DILATIONS) - 1, s, HEAD_DIM), BF16)] * 2
        + [pltpu.VMEM((len(DILATIONS), BLOCK, 2 * BLOCK), F32)]
        + [pltpu.VMEM((len(DILATIONS), s, HEAD_DIM), F32)] * 3,
        compiler_params=_params("parallel", "parallel"),
        name="dilated_attn",
    )(z3, z3, z3, slopes, mix_g)


def _gelu(x):
    return 0.5 * x * (1.0 + lax.erf(x * math.sqrt(0.5)))


def _sgu_body(u_ref, v_ref, ng_ref, w_ref, bt_ref, mg_ref, o_ref, *, tm):
    ti = lax.broadcasted_iota(jnp.int32, (CHUNK, CHUNK), 0)
    si = lax.broadcasted_iota(jnp.int32, (CHUNK, CHUNK), 1)
    causal = si <= ti
    n_ch = tm // CHUNK
    for g0 in range(0, N_GROUPS, SGU_BATCH):
        groups = range(g0, g0 + SGU_BATCH)
        vcat = []
        for g in groups:
            cols = slice(g * HEAD_DIM, (g + 1) * HEAD_DIM)
            ng = ng_ref[:, cols]
            vs = []
            for ch in range(n_ch):
                v = _gelu(v_ref[ch * CHUNK:(ch + 1) * CHUNK, cols].astype(F32))
                v = v * lax.rsqrt(jnp.mean(v * v, axis=-1, keepdims=True) + EPS) * ng
                vs.append(v.astype(BF16))
            vcat.append(jnp.concatenate(vs, axis=1))
        mixed = [jnp.dot(jnp.where(causal, w_ref[g], 0.0).astype(BF16), vc, preferred_element_type=F32)
                 for g, vc in zip(groups, vcat)]
        for g, mx in zip(groups, mixed):
            cols = slice(g * HEAD_DIM, (g + 1) * HEAD_DIM)
            bcol = bt_ref[:, g:g + 1]
            mg = mg_ref[:, cols]
            for ch in range(n_ch):
                rows = slice(ch * CHUNK, (ch + 1) * CHUNK)
                u = _gelu(u_ref[rows, cols].astype(F32))
                y = u * (mx[:, ch * HEAD_DIM:(ch + 1) * HEAD_DIM] + bcol)
                y = y * lax.rsqrt(jnp.mean(y * y, axis=-1, keepdims=True) + EPS) * mg
                o_ref[rows, cols] = y.astype(o_ref.dtype)


def _sgu(z, sgu_norm_g, w_spatial, b_spatial_t, mix_g, tm=256):
    m = z.shape[0]
    ublk = 3 * D_ATTN // D_SGU
    return pl.pallas_call(
        functools.partial(_sgu_body, tm=tm),
        grid=(m // tm,),
        in_specs=[pl.BlockSpec((tm, D_SGU), lambda i: (i, ublk)),
                  pl.BlockSpec((tm, D_SGU), lambda i: (i, ublk + 1)),
                  pl.BlockSpec((1, D_SGU), lambda i: (0, 0)),
                  pl.BlockSpec((N_GROUPS, CHUNK, CHUNK), lambda i: (0, 0, 0)),
                  pl.BlockSpec((CHUNK, N_GROUPS), lambda i: (0, 0)),
                  pl.BlockSpec((1, D_SGU), lambda i: (0, 1))],
        out_specs=pl.BlockSpec((tm, D_SGU), lambda i: (i, 0)),
        out_shape=jax.ShapeDtypeStruct((m, D_SGU), BF16),
        compiler_params=_params("parallel"),
        name="sgu",
    )(z, z, sgu_norm_g.reshape(1, D_SGU), w_spatial, b_spatial_t, mix_g)


def _outproj_body(a_ref, g_ref, wa_ref, wg_ref, x_ref, o_ref):
    acc = jnp.dot(a_ref[...], wa_ref[...].astype(BF16), preferred_element_type=F32)
    acc = acc + jnp.dot(g_ref[...], wg_ref[...].astype(BF16), preferred_element_type=F32)
    o_ref[...] = x_ref[...] + acc


def _outproj(a, g, w, layer, x, tm=1024, tn=512):
    m, ka = a.shape
    kg = g.shape[1]
    assert ka == kg
    n = w.shape[2]
    return pl.pallas_call(
        _outproj_body,
        grid=(m // tm, n // tn),
        in_specs=[pl.BlockSpec((tm, ka), lambda i, j: (i, 0)),
                  pl.BlockSpec((tm, kg), lambda i, j: (i, 0)),
                  pl.BlockSpec((None, ka, tn), lambda i, j: (layer, 0, j)),
                  pl.BlockSpec((None, kg, tn), lambda i, j: (layer, 1, j)),
                  pl.BlockSpec((tm, tn), lambda i, j: (i, j))],
        out_specs=pl.BlockSpec((tm, tn), lambda i, j: (i, j)),
        out_shape=jax.ShapeDtypeStruct((m, n), F32),
        compiler_params=_params("parallel", "arbitrary"),
        name="outproj",
    )(a, g, w, w, x)


def _up_body(h_ref, wg_ref, wv_ref, cwg_ref, cwv_ref, cbg_ref, cbv_ref, wd_ref, o_ref, wdo_ref,
             ext_a, ext_b, carry_ref, *, tm, nj, blocks_per_seq):
    t = pl.program_id(0)
    te = jnp.maximum(t - 1, 0)
    je = te % nj
    seq_start = ((te // nj) % blocks_per_seq) == 0

    @pl.when(t == 0)
    def _():
        ext_a[...] = jnp.zeros_like(ext_a)
        ext_b[...] = jnp.zeros_like(ext_b)
        carry_ref[...] = jnp.zeros_like(carry_ref)

    def stage(ext_w, ext_r):
        for half in range(2):
            slot = half * nj + je
            prev = jnp.where(seq_start, 0.0, carry_ref[slot])
            carry_ref[slot] = ext_r[half, tm:tm + CARRY_ROWS, :]
            ext_r[half, 0:CARRY_ROWS, :] = prev

        def conv(half, r0, cw_ref, cb_ref):
            base = CARRY_ROWS - (CONV_WIDTH - 1) + r0
            y = cb_ref[...] + cw_ref[0:1, :] * ext_r[half, base:base + EPI_ROWS, :]
            y = y + cw_ref[1:2, :] * ext_r[half, base + 1:base + 1 + EPI_ROWS, :]
            return y + cw_ref[2:3, :] * ext_r[half, base + 2:base + 2 + EPI_ROWS, :]

        kc = h_ref.shape[1] // K_SPLIT
        rows_per_piece = tm // (2 * K_SPLIT)
        piece = 0
        for half, w_ref in enumerate((wg_ref, wv_ref)):
            acc = None
            for p in range(K_SPLIT):
                ks = slice(p * kc, (p + 1) * kc)
                part = jnp.dot(h_ref[:, ks], w_ref[ks, :].astype(BF16), preferred_element_type=F32)
                acc = part if acc is None else acc + part
                for r0 in range(piece * rows_per_piece, (piece + 1) * rows_per_piece, EPI_ROWS):
                    gate = conv(0, r0, cwg_ref, cbg_ref)
                    val = conv(1, r0, cwv_ref, cbv_ref)
                    o_ref[r0:r0 + EPI_ROWS, :] = (jax.nn.silu(gate) * val).astype(o_ref.dtype)
                piece += 1
                if piece == K_SPLIT // 2:
                    wdo_ref[...] = wd_ref[...].astype(wdo_ref.dtype)
            ext_w[half, CARRY_ROWS:, :] = acc

    @pl.when(t % 2 == 0)
    def _():
        stage(ext_a, ext_b)

    @pl.when(t % 2 == 1)
    def _():
        stage(ext_b, ext_a)


def _upproj(h, w, conv_w, conv_b, w_down, layer, seq, tm=1024, tn=256):
    m, k = h.shape
    nj = D_FF // tn
    n_tiles = (m // tm) * nj
    n_down = w_down.shape[2]
    slab = D_FF // n_tiles
    assert slab * n_tiles == D_FF and slab % 16 == 0
    body = functools.partial(_up_body, tm=tm, nj=nj, blocks_per_seq=seq // tm)
    mm = lambda t: jnp.minimum(t, n_tiles - 1)
    ep = lambda t: jnp.maximum(t - 1, 0)
    return pl.pallas_call(
        body,
        grid=(n_tiles + 1,),
        in_specs=[pl.BlockSpec((tm, k), lambda t: (mm(t) // nj, 0)),
                  pl.BlockSpec((None, k, tn), lambda t: (layer, 0, mm(t) % nj)),
                  pl.BlockSpec((None, k, tn), lambda t: (layer, 0, mm(t) % nj + nj)),
                  pl.BlockSpec((None, CONV_WIDTH, tn), lambda t: (layer, 0, ep(t) % nj)),
                  pl.BlockSpec((None, CONV_WIDTH, tn), lambda t: (layer, 0, ep(t) % nj + nj)),
                  pl.BlockSpec((None, 1, tn), lambda t: (layer, 0, ep(t) % nj)),
                  pl.BlockSpec((None, 1, tn), lambda t: (layer, 0, ep(t) % nj + nj)),
                  pl.BlockSpec((None, slab, n_down), lambda t: (layer, mm(t), 0))],
        out_specs=[pl.BlockSpec((tm, tn), lambda t: (ep(t) // nj, ep(t) % nj)),
                   pl.BlockSpec((slab, n_down), lambda t: (mm(t), 0))],
        out_shape=[jax.ShapeDtypeStruct((m, D_FF), BF16),
                   jax.ShapeDtypeStruct((D_FF, n_down), BF16)],
        scratch_shapes=[pltpu.VMEM((2, tm + CARRY_ROWS, tn), F32),
                        pltpu.VMEM((2, tm + CARRY_ROWS, tn), F32),
                        pltpu.VMEM((2 * nj, CARRY_ROWS, tn), F32)],
        compiler_params=_params("arbitrary"),
        name="upproj_conv_gate",
    )(h, w, w, conv_w, conv_w, conv_b, conv_b, w_down)


def _down_body(a_ref, w_ref, x_ref, o_ref):
    o_ref[...] = x_ref[...] + jnp.dot(a_ref[...], w_ref[...], preferred_element_type=F32)


def _downproj(a, w, x, tm=512, tn=512):
    m, k = a.shape
    n = w.shape[1]
    return pl.pallas_call(
        _down_body,
        grid=(m // tm, n // tn),
        in_specs=[pl.BlockSpec((tm, k), lambda i, j: (i, 0)),
                  pl.BlockSpec((k, tn), lambda i, j: (0, j)),
                  pl.BlockSpec((tm, tn), lambda i, j: (i, j))],
        out_specs=pl.BlockSpec((tm, tn), lambda i, j: (i, j)),
        out_shape=jax.ShapeDtypeStruct((m, n), F32),
        compiler_params=_params("parallel", "arbitrary"),
        name="downproj",
    )(a, w, x)


def kernel(x, attn_norm_g, w_in, sgu_norm_g, w_spatial, b_spatial, mix_norm_g, w_out, ffn_norm_g,
           w_up, conv_w, conv_b, w_down, final_norm_g):
    b, s, d = x.shape
    depth = w_in.shape[0]
    xm = x.reshape(b * s, d)
    slopes = 2.0 ** (-8.0 * jnp.arange(1, N_HEADS + 1, dtype=F32) / N_HEADS)
    slopes = jnp.broadcast_to(slopes[:, None, None], (N_HEADS, 1, HEAD_DIM))
    conv_b3 = conv_b.reshape(depth, 1, 2 * D_FF)
    for i in range(depth):
        mix_g = mix_norm_g[i].reshape(1, D_MODEL)
        h = _rmsnorm(xm, attn_norm_g[i], BF16)
        z = _inproj(h, w_in, i)
        a = _attention(z.reshape(b, s, N_IN), slopes, mix_g).reshape(b * s, D_ATTN)
        g = _sgu(z, sgu_norm_g[i], w_spatial[i], b_spatial[i].T, mix_g)
        xm = _outproj(a, g, w_out, i, xm)
        h = _rmsnorm(xm, ffn_norm_g[i], BF16)
        act, w_down_bf = _upproj(h, w_up, conv_w, conv_b3, w_down, i, s)
        xm = _downproj(act, w_down_bf, xm)
    return _rmsnorm(xm, final_norm_g, F32).reshape(b, s, d)
```

```python
import functools
import math

import jax
import jax.numpy as jnp
from jax import lax
from jax.experimental import pallas as pl
from jax.experimental.pallas import tpu as pltpu

F32 = jnp.float32
BF16 = jnp.bfloat16

D_MODEL = 4096
HEAD_DIM = 128
D_ATTN = D_MODEL // 2
D_SGU = D_MODEL - D_ATTN
N_HEADS = D_ATTN // HEAD_DIM
N_GROUPS = D_SGU // HEAD_DIM
CHUNK = 128
BLOCK = 128
DILATIONS = (1, 4, 16)
N_IN = 3 * D_ATTN + 2 * D_SGU
D_FF = 256 * ((8 * D_MODEL // 3 + 255) // 256)
CONV_WIDTH = 3
EPS = 1e-6
LOG2E = math.log2(math.e)
VMEM_LIMIT = 61 * 1024 * 1024
CARRY_ROWS = 8
EPI_ROWS = 32
K_SPLIT = 16
SGU_BATCH = 4
ATT_GROUP = 16


def _params(*sem):
    return pltpu.CompilerParams(dimension_semantics=sem, vmem_limit_bytes=VMEM_LIMIT)


def _rmsnorm_body(x_ref, g_ref, o_ref):
    x = x_ref[...]
    y = x * lax.rsqrt(jnp.mean(x * x, axis=-1, keepdims=True) + EPS)
    o_ref[...] = (y * g_ref[...]).astype(o_ref.dtype)


def _rmsnorm(x, g, out_dtype, tm=256):
    m, d = x.shape
    return pl.pallas_call(
        _rmsnorm_body,
        grid=(m // tm,),
        in_specs=[pl.BlockSpec((tm, d), lambda i: (i, 0)),
                  pl.BlockSpec((1, d), lambda i: (0, 0))],
        out_specs=pl.BlockSpec((tm, d), lambda i: (i, 0)),
        out_shape=jax.ShapeDtypeStruct((m, d), out_dtype),
        compiler_params=_params("parallel"),
        name="rmsnorm",
    )(x, g.reshape(1, d))


def _inproj_body(h_ref, w_ref, o_ref):
    w = w_ref[...].astype(BF16)
    o_ref[...] = jnp.dot(h_ref[...], w, preferred_element_type=F32).astype(o_ref.dtype)


def _inproj(h, w, layer, tm=1024, tn=1024):
    m, k = h.shape
    n = w.shape[2]
    return pl.pallas_call(
        _inproj_body,
        grid=(m // tm, n // tn),
        in_specs=[pl.BlockSpec((tm, k), lambda i, j: (i, 0)),
                  pl.BlockSpec((None, k, tn), lambda i, j: (layer, 0, j))],
        out_specs=pl.BlockSpec((tm, tn), lambda i, j: (i, j)),
        out_shape=jax.ShapeDtypeStruct((m, n), BF16),
        compiler_params=_params("parallel", "arbitrary"),
        name="inproj",
    )(h, w)


def _aligned(i, m):
    return i if isinstance(i, int) else pl.multiple_of(i, m)


def _attn_body(q_ref, k_ref, v_ref, slope_ref, g_ref, o_ref, xf, x4f, qd, kd, vd, bias2, ob, mb, lb,
               *, seq):
    qscale = HEAD_DIM ** -0.5 * LOG2E
    d1, d2 = DILATIONS[1], DILATIONS[2]
    assert d2 == d1 * d1 and DILATIONS[0] == 1
    sub1, sub2 = seq // d1, seq // d2
    for src, dst, mul, first_c in ((q_ref, qd, qscale, 0), (k_ref, kd, None, 1), (v_ref, vd, None, 1)):
        if first_c == 0:
            xf[...] = src[...].astype(F32) * mul
            dst[0] = xf[...].astype(BF16)
        else:
            xf[...] = src[...].astype(F32)

        def deint1(r, carry, dst=dst, slot=1 - first_c):
            rows = pl.ds(pl.multiple_of(r * sub1, sub1), sub1)
            x = xf[pl.ds(r, sub1, stride=d1), :]
            x4f[rows, :] = x
            dst[slot, rows, :] = x.astype(BF16)
            return carry

        lax.fori_loop(0, d1, deint1, 0)

        def deint2(r, carry, dst=dst, slot=2 - first_c):
            rows = pl.ds(pl.multiple_of(r * sub2, sub2), sub2)
            x = x4f[pl.ds((r % d1) * sub1 + r // d1, sub2, stride=d1), :]
            dst[slot, rows, :] = x.astype(BF16)
            return carry

        lax.fori_loop(0, d2, deint2, 0)

    slope = slope_ref[...]
    qi = lax.broadcasted_iota(jnp.int32, (BLOCK, 2 * BLOCK), 0)
    kj = lax.broadcasted_iota(jnp.int32, (BLOCK, 2 * BLOCK), 1)
    dist = qi + BLOCK - kj
    valid = (dist >= 0) & (dist <= BLOCK)
    for c, d in enumerate(DILATIONS):
        bias2[c] = jnp.where(valid, (-LOG2E * slope[:, :1]) * (dist * d).astype(F32), -jnp.inf)
    ones = jnp.ones((2 * BLOCK, HEAD_DIM), BF16)

    def group(c, d, blocks):
        getk = (lambda rows: k_ref[rows, :]) if c == 0 else (lambda rows: kd[c - 1, rows, :])
        getv = (lambda rows: v_ref[rows, :]) if c == 0 else (lambda rows: vd[c - 1, rows, :])
        qs, ks, vs = [], [], []
        for gi, _, first in blocks:
            qrows = pl.ds(_aligned(gi * BLOCK, BLOCK), BLOCK)
            krows = qrows if first else pl.ds(_aligned(gi * BLOCK - BLOCK, BLOCK), 2 * BLOCK)
            qs.append(qd[c, qrows, :])
            ks.append(getk(krows))
            v = getv(krows)
            vs.append(jnp.concatenate([v, ones[:v.shape[0]]], axis=1))
        ss = [lax.dot_general(q, k, (((1,), (1,)), ((), ())), preferred_element_type=F32)
              for q, k in zip(qs, ks)]
        ps, ms = [], []
        for s, (_, _, first) in zip(ss, blocks):
            s = s + (bias2[c, :, BLOCK:] if first else bias2[c])
            mx = jnp.max(s, axis=-1, keepdims=True)
            ms.append(mx)
            ps.append(jnp.exp2(s - mx).astype(BF16))
        os = [jnp.dot(p, v, preferred_element_type=F32) for p, v in zip(ps, vs)]
        for o, mx, (_, rows, _) in zip(os, ms, blocks):
            ob[c, rows, :] = o[:, :HEAD_DIM]
            lb[c, rows, :] = o[:, HEAD_DIM:]
            mb[c, rows, :] = jnp.broadcast_to(mx, (BLOCK, HEAD_DIM))

    for c, d in enumerate(DILATIONS):
        nb = seq // d // BLOCK

        def blk(r, n, first, d=d, nb=nb):
            gi = r * nb + n
            if d <= d1:
                return (gi, pl.ds(_aligned(gi * BLOCK, BLOCK), BLOCK), first)
            q = d // d1
            start = (r % d1) * sub1 + q * BLOCK * n + r // d1
            return (gi, pl.ds(start, BLOCK, stride=q), first)

        if nb >= ATT_GROUP:
            def r_body(r, carry, c=c, d=d, nb=nb, blk=blk):
                group(c, d, [blk(r, n, n == 0) for n in range(ATT_GROUP)])

                def n_body(i, carry2):
                    group(c, d, [blk(r, i * ATT_GROUP + t, False) for t in range(ATT_GROUP)])
                    return carry2

                return lax.fori_loop(1, nb // ATT_GROUP, n_body, carry)

            if d == 1:
                r_body(0, 0)
            else:
                lax.fori_loop(0, d, r_body, 0)
        else:
            per = ATT_GROUP // nb

            def r_body(i, carry, c=c, d=d, nb=nb, blk=blk, per=per):
                group(c, d, [blk(i * per + t, n, n == 0) for t in range(per) for n in range(nb)])
                return carry

            lax.fori_loop(0, d // per, r_body, 0)

    g = g_ref[...]
    step, piece = 512, 64

    def fin(t, carry):
        r = (t * step) // sub1
        m0 = (t * step) % sub1
        for p0 in range(0, step, piece):
            rows = pl.ds(pl.multiple_of(t * step + p0, piece), piece)
            toks = pl.ds(r + d1 * (m0 + p0), piece, stride=d1)
            ma, mb1, mb2 = mb[0, toks, :], mb[1, rows, :], mb[2, rows, :]
            mx = jnp.maximum(jnp.maximum(ma, mb1), mb2)
            e0, e1, e2 = jnp.exp2(ma - mx), jnp.exp2(mb1 - mx), jnp.exp2(mb2 - mx)
            den = e0 * lb[0, toks, :] + e1 * lb[1, rows, :] + e2 * lb[2, rows, :]
            num = e0 * ob[0, toks, :] + e1 * ob[1, rows, :] + e2 * ob[2, rows, :]
            msq = jnp.mean(num * num, axis=-1, keepdims=True)
            xf[toks, :] = num * lax.rsqrt(msq + EPS * (den * den)) * g
        return carry

    lax.fori_loop(0, seq // step, fin, 0)
    o_ref[...] = xf[...].astype(o_ref.dtype)


def _attention(z3, slopes, mix_g):
    b, s, _ = z3.shape
    blk = lambda off: pl.BlockSpec((None, s, HEAD_DIM), lambda bi, h, off=off: (bi, 0, off + h))
    return pl.pallas_call(
        functools.partial(_attn_body, seq=s),
        grid=(b, N_HEADS),
        in_specs=[blk(0), blk(N_HEADS), blk(2 * N_HEADS),
                  pl.BlockSpec((None, 1, HEAD_DIM), lambda bi, h: (h, 0, 0)),
                  pl.BlockSpec((1, HEAD_DIM), lambda bi, h: (0, h))],
        out_specs=pl.BlockSpec((None, s, HEAD_DIM), lambda bi, h: (bi, 0, h)),
        out_shape=jax.ShapeDtypeStruct((b, s, D_ATTN), BF16),
        scratch_shapes=[pltpu.VMEM((s, HEAD_DIM), F32)] * 2
        + [pltpu.VMEM((len(DILATIONS), s, HEAD_DIM), BF16)]
        + [pltpu.VMEM((len(DILATIONS) - 1, s, HEAD_DIM), BF16)] * 2
        + [pltpu.VMEM((len(DILATIONS), BLOCK, 2 * BLOCK), F32)]
        + [pltpu.VMEM((len(DILATIONS), s, HEAD_DIM), F32)] * 3,
        compiler_params=_params("parallel", "parallel"),
        name="dilated_attn",
    )(z3, z3, z3, slopes, mix_g)


def _gelu(x):
    return 0.5 * x * (1.0 + lax.erf(x * math.sqrt(0.5)))


def _sgu_body(u_ref, v_ref, ng_ref, w_ref, bt_ref, mg_ref, o_ref, *, tm):
    ti = lax.broadcasted_iota(jnp.int32, (CHUNK, CHUNK), 0)
    si = lax.broadcasted_iota(jnp.int32, (CHUNK, CHUNK), 1)
    causal = si <= ti
    n_ch = tm // CHUNK
    for g0 in range(0, N_GROUPS, SGU_BATCH):
        groups = range(g0, g0 + SGU_BATCH)
        vcat = []
        for g in groups:
            cols = slice(g * HEAD_DIM, (g + 1) * HEAD_DIM)
            ng = ng_ref[:, cols]
            vs = []
            for ch in range(n_ch):
                v = _gelu(v_ref[ch * CHUNK:(ch + 1) * CHUNK, cols].astype(F32))
                v = v * lax.rsqrt(jnp.mean(v * v, axis=-1, keepdims=True) + EPS) * ng
                vs.append(v.astype(BF16))
            vcat.append(jnp.concatenate(vs, axis=1))
        mixed = [jnp.dot(jnp.where(causal, w_ref[g], 0.0).astype(BF16), vc, preferred_element_type=F32)
                 for g, vc in zip(groups, vcat)]
        for g, mx in zip(groups, mixed):
            cols = slice(g * HEAD_DIM, (g + 1) * HEAD_DIM)
            bcol = bt_ref[:, g:g + 1]
            mg = mg_ref[:, cols]
            for ch in range(n_ch):
                rows = slice(ch * CHUNK, (ch + 1) * CHUNK)
                u = _gelu(u_ref[rows, cols].astype(F32))
                y = u * (mx[:, ch * HEAD_DIM:(ch + 1) * HEAD_DIM] + bcol)
                y = y * lax.rsqrt(jnp.mean(y * y, axis=-1, keepdims=True) + EPS) * mg
                o_ref[rows, cols] = y.astype(o_ref.dtype)


def _sgu(z, sgu_norm_g, w_spatial, b_spatial_t, mix_g, tm=256):
    m = z.shape[0]
    ublk = 3 * D_ATTN // D_SGU
    return pl.pallas_call(
        functools.partial(_sgu_body, tm=tm),
        grid=(m // tm,),
        in_specs=[pl.BlockSpec((tm, D_SGU), lambda i: (i, ublk)),
                  pl.BlockSpec((tm, D_SGU), lambda i: (i, ublk + 1)),
                  pl.BlockSpec((1, D_SGU), lambda i: (0, 0)),
                  pl.BlockSpec((N_GROUPS, CHUNK, CHUNK), lambda i: (0, 0, 0)),
                  pl.BlockSpec((CHUNK, N_GROUPS), lambda i: (0, 0)),
                  pl.BlockSpec((1, D_SGU), lambda i: (0, 1))],
        out_specs=pl.BlockSpec((tm, D_SGU), lambda i: (i, 0)),
        out_shape=jax.ShapeDtypeStruct((m, D_SGU), BF16),
        compiler_params=_params("parallel"),
        name="sgu",
    )(z, z, sgu_norm_g.reshape(1, D_SGU), w_spatial, b_spatial_t, mix_g)


def _outproj_body(a_ref, g_ref, wa_ref, wg_ref, x_ref, o_ref):
    acc = jnp.dot(a_ref[...], wa_ref[...].astype(BF16), preferred_element_type=F32)
    acc = acc + jnp.dot(g_ref[...], wg_ref[...].astype(BF16), preferred_element_type=F32)
    o_ref[...] = x_ref[...] + acc


def _outproj(a, g, w, layer, x, tm=1024, tn=512):
    m, ka = a.shape
    kg = g.shape[1]
    assert ka == kg
    n = w.shape[2]
    return pl.pallas_call(
        _outproj_body,
        grid=(m // tm, n // tn),
        in_specs=[pl.BlockSpec((tm, ka), lambda i, j: (i, 0)),
                  pl.BlockSpec((tm, kg), lambda i, j: (i, 0)),
                  pl.BlockSpec((None, ka, tn), lambda i, j: (layer, 0, j)),
                  pl.BlockSpec((None, kg, tn), lambda i, j: (layer, 1, j)),
                  pl.BlockSpec((tm, tn), lambda i, j: (i, j))],
        out_specs=pl.BlockSpec((tm, tn), lambda i, j: (i, j)),
        out_shape=jax.ShapeDtypeStruct((m, n), F32),
        compiler_params=_params("parallel", "arbitrary"),
        name="outproj",
    )(a, g, w, w, x)


def _up_body(h_ref, wg_ref, wv_ref, cwg_ref, cwv_ref, cbg_ref, cbv_ref, wd_ref, o_ref, wdo_ref,
             ext_a, ext_b, carry_ref, *, tm, nj, blocks_per_seq):
    t = pl.program_id(0)
    te = jnp.maximum(t - 1, 0)
    je = te % nj
    seq_start = ((te // nj) % blocks_per_seq) == 0

    @pl.when(t == 0)
    def _():
        ext_a[...] = jnp.zeros_like(ext_a)
        ext_b[...] = jnp.zeros_like(ext_b)
        carry_ref[...] = jnp.zeros_like(carry_ref)

    def stage(ext_w, ext_r):
        for half in range(2):
            slot = half * nj + je
            prev = jnp.where(seq_start, 0.0, carry_ref[slot])
            carry_ref[slot] = ext_r[half, tm:tm + CARRY_ROWS, :]
            ext_r[half, 0:CARRY_ROWS, :] = prev

        def conv(half, r0, cw_ref, cb_ref):
            slab = ext_r[half, r0:r0 + CARRY_ROWS + EPI_ROWS, :]
            y = cb_ref[...]
            for j in range(CONV_WIDTH - 1, 0, -1):
                tap = pltpu.roll(slab, j, axis=0)[CARRY_ROWS:, :]
                y = y + cw_ref[CONV_WIDTH - 1 - j:CONV_WIDTH - j, :] * tap
            return y + cw_ref[CONV_WIDTH - 1:CONV_WIDTH, :] * slab[CARRY_ROWS:, :]

        kc = h_ref.shape[1] // K_SPLIT
        rows_per_piece = tm // (2 * K_SPLIT)
        piece = 0
        for half, w_ref in enumerate((wg_ref, wv_ref)):
            acc = None
            for p in range(K_SPLIT):
                ks = slice(p * kc, (p + 1) * kc)
                part = jnp.dot(h_ref[:, ks], w_ref[ks, :].astype(BF16), preferred_element_type=F32)
                acc = part if acc is None else acc + part
                for r0 in range(piece * rows_per_piece, (piece + 1) * rows_per_piece, EPI_ROWS):
                    gate = conv(0, r0, cwg_ref, cbg_ref)
                    val = conv(1, r0, cwv_ref, cbv_ref)
                    o_ref[r0:r0 + EPI_ROWS, :] = (jax.nn.silu(gate) * val).astype(o_ref.dtype)
                piece += 1
                if piece == K_SPLIT // 2:
                    wdo_ref[...] = wd_ref[...].astype(wdo_ref.dtype)
            ext_w[half, CARRY_ROWS:, :] = acc

    @pl.when(t % 2 == 0)
    def _():
        stage(ext_a, ext_b)

    @pl.when(t % 2 == 1)
    def _():
        stage(ext_b, ext_a)


def _upproj(h, w, conv_w, conv_b, w_down, layer, seq, tm=1024, tn=256):
    m, k = h.shape
    nj = D_FF // tn
    n_tiles = (m // tm) * nj
    n_down = w_down.shape[2]
    slab = D_FF // n_tiles
    assert slab * n_tiles == D_FF and slab % 16 == 0
    body = functools.partial(_up_body, tm=tm, nj=nj, blocks_per_seq=seq // tm)
    mm = lambda t: jnp.minimum(t, n_tiles - 1)
    ep = lambda t: jnp.maximum(t - 1, 0)
    return pl.pallas_call(
        body,
        grid=(n_tiles + 1,),
        in_specs=[pl.BlockSpec((tm, k), lambda t: (mm(t) // nj, 0)),
                  pl.BlockSpec((None, k, tn), lambda t: (layer, 0, mm(t) % nj)),
                  pl.BlockSpec((None, k, tn), lambda t: (layer, 0, mm(t) % nj + nj)),
                  pl.BlockSpec((None, CONV_WIDTH, tn), lambda t: (layer, 0, ep(t) % nj)),
                  pl.BlockSpec((None, CONV_WIDTH, tn), lambda t: (layer, 0, ep(t) % nj + nj)),
                  pl.BlockSpec((None, 1, tn), lambda t: (layer, 0, ep(t) % nj)),
                  pl.BlockSpec((None, 1, tn), lambda t: (layer, 0, ep(t) % nj + nj)),
                  pl.BlockSpec((None, slab, n_down), lambda t: (layer, mm(t), 0))],
        out_specs=[pl.BlockSpec((tm, tn), lambda t: (ep(t) // nj, ep(t) % nj)),
                   pl.BlockSpec((slab, n_down), lambda t: (mm(t), 0))],
        out_shape=[jax.ShapeDtypeStruct((m, D_FF), BF16),
                   jax.ShapeDtypeStruct((D_FF, n_down), BF16)],
        scratch_shapes=[pltpu.VMEM((2, tm + CARRY_ROWS, tn), F32),
                        pltpu.VMEM((2, tm + CARRY_ROWS, tn), F32),
                        pltpu.VMEM((2 * nj, CARRY_ROWS, tn), F32)],
        compiler_params=_params("arbitrary"),
        name="upproj_conv_gate",
    )(h, w, w, conv_w, conv_w, conv_b, conv_b, w_down)


def _down_body(a_ref, w_ref, x_ref, o_ref):
    o_ref[...] = x_ref[...] + jnp.dot(a_ref[...], w_ref[...], preferred_element_type=F32)


def _downproj(a, w, x, tm=512, tn=512):
    m, k = a.shape
    n = w.shape[1]
    return pl.pallas_call(
        _down_body,
        grid=(m // tm, n // tn),
        in_specs=[pl.BlockSpec((tm, k), lambda i, j: (i, 0)),
                  pl.BlockSpec((k, tn), lambda i, j: (0, j)),
                  pl.BlockSpec((tm, tn), lambda i, j: (i, j))],
        out_specs=pl.BlockSpec((tm, tn), lambda i, j: (i, j)),
        out_shape=jax.ShapeDtypeStruct((m, n), F32),
        compiler_params=_params("parallel", "arbitrary"),
        name="downproj",
    )(a, w, x)


def kernel(x, attn_norm_g, w_in, sgu_norm_g, w_spatial, b_spatial, mix_norm_g, w_out, ffn_norm_g,
           w_up, conv_w, conv_b, w_down, final_norm_g):
    b, s, d = x.shape
    depth = w_in.shape[0]
    xm = x.reshape(b * s, d)
    slopes = 2.0 ** (-8.0 * jnp.arange(1, N_HEADS + 1, dtype=F32) / N_HEADS)
    slopes = jnp.broadcast_to(slopes[:, None, None], (N_HEADS, 1, HEAD_DIM))
    conv_b3 = conv_b.reshape(depth, 1, 2 * D_FF)
    for i in range(depth):
        mix_g = mix_norm_g[i].reshape(1, D_MODEL)
        h = _rmsnorm(xm, attn_norm_g[i], BF16)
        z = _inproj(h, w_in, i)
        a = _attention(z.reshape(b, s, N_IN), slopes, mix_g).reshape(b * s, D_ATTN)
        g = _sgu(z, sgu_norm_g[i], w_spatial[i], b_spatial[i].T, mix_g)
        xm = _outproj(a, g, w_out, i, xm)
        h = _rmsnorm(xm, ffn_norm_g[i], BF16)
        act, w_down_bf = _upproj(h, w_up, conv_w, conv_b3, w_down, i, s)
        xm = _downproj(act, w_down_bf, xm)
    return _rmsnorm(xm, final_norm_g, F32).reshape(b, s, d)
```

```python
import functools
import math

import jax
import jax.numpy as jnp
from jax import lax
from jax.experimental import pallas as pl
from jax.experimental.pallas import tpu as pltpu

F32 = jnp.float32
BF16 = jnp.bfloat16

D_MODEL = 4096
HEAD_DIM = 128
D_ATTN = D_MODEL // 2
D_SGU = D_MODEL - D_ATTN
N_HEADS = D_ATTN // HEAD_DIM
N_GROUPS = D_SGU // HEAD_DIM
CHUNK = 128
BLOCK = 128
DILATIONS = (1, 4, 16)
N_IN = 3 * D_ATTN + 2 * D_SGU
D_FF = 256 * ((8 * D_MODEL // 3 + 255) // 256)
CONV_WIDTH = 3
EPS = 1e-6
LOG2E = math.log2(math.e)
VMEM_LIMIT = 61 * 1024 * 1024
CARRY_ROWS = 8
EPI_ROWS = 32
K_SPLIT = 16
NORM_ROWS = 32
SGU_BATCH = 4
ATT_GROUP = 16


def _params(*sem):
    return pltpu.CompilerParams(dimension_semantics=sem, vmem_limit_bytes=VMEM_LIMIT)


def _rmsnorm_body(x_ref, g_ref, o_ref):
    x = x_ref[...]
    y = x * lax.rsqrt(jnp.mean(x * x, axis=-1, keepdims=True) + EPS)
    o_ref[...] = (y * g_ref[...]).astype(o_ref.dtype)


def _rmsnorm(x, g, out_dtype, tm=256, rows=None):
    m, d = x.shape
    m = m if rows is None else rows
    return pl.pallas_call(
        _rmsnorm_body,
        grid=(m // tm,),
        in_specs=[pl.BlockSpec((tm, d), lambda i: (i, 0)),
                  pl.BlockSpec((1, d), lambda i: (0, 0))],
        out_specs=pl.BlockSpec((tm, d), lambda i: (i, 0)),
        out_shape=jax.ShapeDtypeStruct((m, d), out_dtype),
        compiler_params=_params("parallel"),
        name="rmsnorm",
    )(x, g.reshape(1, d))


def _inproj_body(h_ref, w_ref, o_ref):
    w = w_ref[...].astype(BF16)
    o_ref[...] = jnp.dot(h_ref[...], w, preferred_element_type=F32).astype(o_ref.dtype)


def _inproj(h, w, layer, tm=1024, tn=1024):
    m, k = h.shape
    n = w.shape[2]
    return pl.pallas_call(
        _inproj_body,
        grid=(m // tm, n // tn),
        in_specs=[pl.BlockSpec((tm, k), lambda i, j: (i, 0)),
                  pl.BlockSpec((None, k, tn), lambda i, j: (layer, 0, j))],
        out_specs=pl.BlockSpec((tm, tn), lambda i, j: (i, j)),
        out_shape=jax.ShapeDtypeStruct((m, n), BF16),
        compiler_params=_params("parallel", "arbitrary"),
        name="inproj",
    )(h, w)


def _aligned(i, m):
    return i if isinstance(i, int) else pl.multiple_of(i, m)


def _attn_body(q_ref, k_ref, v_ref, slope_ref, g_ref, o_ref, xf, x4f, qd, kd, vd, bias2, ob, mb, lb,
               *, seq):
    qscale = HEAD_DIM ** -0.5 * LOG2E
    d1, d2 = DILATIONS[1], DILATIONS[2]
    assert d2 == d1 * d1 and DILATIONS[0] == 1
    sub1, sub2 = seq // d1, seq // d2
    for src, dst, mul, first_c in ((q_ref, qd, qscale, 0), (k_ref, kd, None, 1), (v_ref, vd, None, 1)):
        if first_c == 0:
            xf[...] = src[...].astype(F32) * mul
            dst[0] = xf[...].astype(BF16)
        else:
            xf[...] = src[...].astype(F32)

        def deint1(r, carry, dst=dst, slot=1 - first_c):
            rows = pl.ds(pl.multiple_of(r * sub1, sub1), sub1)
            x = xf[pl.ds(r, sub1, stride=d1), :]
            x4f[rows, :] = x
            dst[slot, rows, :] = x.astype(BF16)
            return carry

        lax.fori_loop(0, d1, deint1, 0)

        def deint2(r, carry, dst=dst, slot=2 - first_c):
            rows = pl.ds(pl.multiple_of(r * sub2, sub2), sub2)
            x = x4f[pl.ds((r % d1) * sub1 + r // d1, sub2, stride=d1), :]
            dst[slot, rows, :] = x.astype(BF16)
            return carry

        lax.fori_loop(0, d2, deint2, 0)

    slope = slope_ref[...]
    qi = lax.broadcasted_iota(jnp.int32, (BLOCK, 2 * BLOCK), 0)
    kj = lax.broadcasted_iota(jnp.int32, (BLOCK, 2 * BLOCK), 1)
    dist = qi + BLOCK - kj
    valid = (dist >= 0) & (dist <= BLOCK)
    for c, d in enumerate(DILATIONS):
        bias2[c] = jnp.where(valid, (-LOG2E * slope[:, :1]) * (dist * d).astype(F32), -jnp.inf)
    ones = jnp.ones((2 * BLOCK, HEAD_DIM), BF16)

    def group(c, d, blocks):
        getk = (lambda rows: k_ref[rows, :]) if c == 0 else (lambda rows: kd[c - 1, rows, :])
        getv = (lambda rows: v_ref[rows, :]) if c == 0 else (lambda rows: vd[c - 1, rows, :])
        qs, ks, vs = [], [], []
        for gi, _, first in blocks:
            qrows = pl.ds(_aligned(gi * BLOCK, BLOCK), BLOCK)
            krows = qrows if first else pl.ds(_aligned(gi * BLOCK - BLOCK, BLOCK), 2 * BLOCK)
            qs.append(qd[c, qrows, :])
            ks.append(getk(krows))
            v = getv(krows)
            vs.append(jnp.concatenate([v, ones[:v.shape[0]]], axis=1))
        ss = [lax.dot_general(q, k, (((1,), (1,)), ((), ())), preferred_element_type=F32)
              for q, k in zip(qs, ks)]
        ps, ms = [], []
        for s, (_, _, first) in zip(ss, blocks):
            s = s + (bias2[c, :, BLOCK:] if first else bias2[c])
            mx = jnp.max(s, axis=-1, keepdims=True)
            ms.append(mx)
            ps.append(jnp.exp2(s - mx).astype(BF16))
        os = [jnp.dot(p, v, preferred_element_type=F32) for p, v in zip(ps, vs)]
        for o, mx, (_, rows, _) in zip(os, ms, blocks):
            ob[c, rows, :] = o[:, :HEAD_DIM]
            lb[c, rows, :] = o[:, HEAD_DIM:]
            mb[c, rows, :] = jnp.broadcast_to(mx, (BLOCK, HEAD_DIM))

    for c, d in enumerate(DILATIONS):
        nb = seq // d // BLOCK

        def blk(r, n, first, d=d, nb=nb):
            gi = r * nb + n
            if d <= d1:
                return (gi, pl.ds(_aligned(gi * BLOCK, BLOCK), BLOCK), first)
            q = d // d1
            start = (r % d1) * sub1 + q * BLOCK * n + r // d1
            return (gi, pl.ds(start, BLOCK, stride=q), first)

        if nb >= ATT_GROUP:
            def r_body(r, carry, c=c, d=d, nb=nb, blk=blk):
                group(c, d, [blk(r, n, n == 0) for n in range(ATT_GROUP)])

                def n_body(i, carry2):
                    group(c, d, [blk(r, i * ATT_GROUP + t, False) for t in range(ATT_GROUP)])
                    return carry2

                return lax.fori_loop(1, nb // ATT_GROUP, n_body, carry)

            if d == 1:
                r_body(0, 0)
            else:
                lax.fori_loop(0, d, r_body, 0)
        else:
            per = ATT_GROUP // nb

            def r_body(i, carry, c=c, d=d, nb=nb, blk=blk, per=per):
                group(c, d, [blk(i * per + t, n, n == 0) for t in range(per) for n in range(nb)])
                return carry

            lax.fori_loop(0, d // per, r_body, 0)

    g = g_ref[...]
    step, piece = 512, 64

    def fin(t, carry):
        r = (t * step) // sub1
        m0 = (t * step) % sub1
        for p0 in range(0, step, piece):
            rows = pl.ds(pl.multiple_of(t * step + p0, piece), piece)
            toks = pl.ds(r + d1 * (m0 + p0), piece, stride=d1)
            ma, mb1, mb2 = mb[0, toks, :], mb[1, rows, :], mb[2, rows, :]
            mx = jnp.maximum(jnp.maximum(ma, mb1), mb2)
            e0, e1, e2 = jnp.exp2(ma - mx), jnp.exp2(mb1 - mx), jnp.exp2(mb2 - mx)
            den = e0 * lb[0, toks, :] + e1 * lb[1, rows, :] + e2 * lb[2, rows, :]
            num = e0 * ob[0, toks, :] + e1 * ob[1, rows, :] + e2 * ob[2, rows, :]
            msq = jnp.mean(num * num, axis=-1, keepdims=True)
            xf[toks, :] = num * lax.rsqrt(msq + EPS * (den * den)) * g
        return carry

    lax.fori_loop(0, seq // step, fin, 0)
    o_ref[...] = xf[...].astype(o_ref.dtype)


def _attention(z3, slopes, mix_g):
    b, s, _ = z3.shape
    blk = lambda off: pl.BlockSpec((None, s, HEAD_DIM), lambda bi, h, off=off: (bi, 0, off + h))
    return pl.pallas_call(
        functools.partial(_attn_body, seq=s),
        grid=(b, N_HEADS),
        in_specs=[blk(0), blk(N_HEADS), blk(2 * N_HEADS),
                  pl.BlockSpec((None, 1, HEAD_DIM), lambda bi, h: (h, 0, 0)),
                  pl.BlockSpec((1, HEAD_DIM), lambda bi, h: (0, h))],
        out_specs=pl.BlockSpec((None, s, HEAD_DIM), lambda bi, h: (bi, 0, h)),
        out_shape=jax.ShapeDtypeStruct((b, s, D_ATTN), BF16),
        scratch_shapes=[pltpu.VMEM((s, HEAD_DIM), F32)] * 2
        + [pltpu.VMEM((len(DILATIONS), s, HEAD_DIM), BF16)]
        + [pltpu.VMEM((len(DILATIONS) - 1, s, HEAD_DIM), BF16)] * 2
        + [pltpu.VMEM((len(DILATIONS), BLOCK, 2 * BLOCK), F32)]
        + [pltpu.VMEM((len(DILATIONS), s, HEAD_DIM), F32)] * 3,
        compiler_params=_params("parallel", "parallel"),
        name="dilated_attn",
    )(z3, z3, z3, slopes, mix_g)


def _gelu(x):
    return 0.5 * x * (1.0 + lax.erf(x * math.sqrt(0.5)))


def _sgu_body(u_ref, v_ref, ng_ref, w_ref, bt_ref, mg_ref, o_ref, *, tm):
    ti = lax.broadcasted_iota(jnp.int32, (CHUNK, CHUNK), 0)
    si = lax.broadcasted_iota(jnp.int32, (CHUNK, CHUNK), 1)
    causal = si <= ti
    n_ch = tm // CHUNK
    for g0 in range(0, N_GROUPS, SGU_BATCH):
        groups = range(g0, g0 + SGU_BATCH)
        vcat = []
        for g in groups:
            cols = slice(g * HEAD_DIM, (g + 1) * HEAD_DIM)
            ng = ng_ref[:, cols]
            vs = []
            for ch in range(n_ch):
                v = _gelu(v_ref[ch * CHUNK:(ch + 1) * CHUNK, cols].astype(F32))
                v = v * lax.rsqrt(jnp.mean(v * v, axis=-1, keepdims=True) + EPS) * ng
                vs.append(v.astype(BF16))
            vcat.append(jnp.concatenate(vs, axis=1))
        mixed = [jnp.dot(jnp.where(causal, w_ref[g], 0.0).astype(BF16), vc, preferred_element_type=F32)
                 for g, vc in zip(groups, vcat)]
        for g, mx in zip(groups, mixed):
            cols = slice(g * HEAD_DIM, (g + 1) * HEAD_DIM)
            bcol = bt_ref[:, g:g + 1]
            mg = mg_ref[:, cols]
            for ch in range(n_ch):
                rows = slice(ch * CHUNK, (ch + 1) * CHUNK)
                u = _gelu(u_ref[rows, cols].astype(F32))
                y = u * (mx[:, ch * HEAD_DIM:(ch + 1) * HEAD_DIM] + bcol)
                y = y * lax.rsqrt(jnp.mean(y * y, axis=-1, keepdims=True) + EPS) * mg
                o_ref[rows, cols] = y.astype(o_ref.dtype)


def _sgu(z, sgu_norm_g, w_spatial, b_spatial_t, mix_g, tm=256):
    m = z.shape[0]
    ublk = 3 * D_ATTN // D_SGU
    return pl.pallas_call(
        functools.partial(_sgu_body, tm=tm),
        grid=(m // tm,),
        in_specs=[pl.BlockSpec((tm, D_SGU), lambda i: (i, ublk)),
                  pl.BlockSpec((tm, D_SGU), lambda i: (i, ublk + 1)),
                  pl.BlockSpec((1, D_SGU), lambda i: (0, 0)),
                  pl.BlockSpec((N_GROUPS, CHUNK, CHUNK), lambda i: (0, 0, 0)),
                  pl.BlockSpec((CHUNK, N_GROUPS), lambda i: (0, 0)),
                  pl.BlockSpec((1, D_SGU), lambda i: (0, 1))],
        out_specs=pl.BlockSpec((tm, D_SGU), lambda i: (i, 0)),
        out_shape=jax.ShapeDtypeStruct((m, D_SGU), BF16),
        compiler_params=_params("parallel"),
        name="sgu",
    )(z, z, sgu_norm_g.reshape(1, D_SGU), w_spatial, b_spatial_t, mix_g)


def _outproj_body(a_ref, g_ref, wa_ref, wg_ref, x_ref, o_ref):
    acc = jnp.dot(a_ref[...], wa_ref[...].astype(BF16), preferred_element_type=F32)
    acc = acc + jnp.dot(g_ref[...], wg_ref[...].astype(BF16), preferred_element_type=F32)
    o_ref[...] = x_ref[...] + acc


def _outproj(a, g, w, layer, x, tm=1024, tn=512):
    m, ka = a.shape
    kg = g.shape[1]
    assert ka == kg
    n = w.shape[2]
    return pl.pallas_call(
        _outproj_body,
        grid=(m // tm, n // tn),
        in_specs=[pl.BlockSpec((tm, ka), lambda i, j: (i, 0)),
                  pl.BlockSpec((tm, kg), lambda i, j: (i, 0)),
                  pl.BlockSpec((None, ka, tn), lambda i, j: (layer, 0, j)),
                  pl.BlockSpec((None, kg, tn), lambda i, j: (layer, 1, j)),
                  pl.BlockSpec((tm, tn), lambda i, j: (i, j))],
        out_specs=pl.BlockSpec((tm, tn), lambda i, j: (i, j)),
        out_shape=jax.ShapeDtypeStruct((m, n), F32),
        compiler_params=_params("parallel", "arbitrary"),
        name="outproj",
    )(a, g, w, w, x)


def _up_body(x_ref, g_ref, h0_ref, wg_ref, wv_ref, cwg_ref, cwv_ref, cbg_ref, cbv_ref, wd_ref,
             o_ref, wdo_ref, h_a, h_b, h_stage, ext_a, ext_b, carry_ref, *, tm, nj, n_tiles, blocks_per_seq):
    t = pl.program_id(0)
    te = jnp.maximum(t - 1, 0)
    je = te % nj
    seq_start = ((te // nj) % blocks_per_seq) == 0
    tmm = jnp.minimum(t, n_tiles - 1)
    cur = (tmm // nj) % 2
    jn = jnp.minimum(tmm % nj, tm // NORM_ROWS - 1)

    @pl.when(t == 0)
    def _():
        ext_a[...] = jnp.zeros_like(ext_a)
        ext_b[...] = jnp.zeros_like(ext_b)
        carry_ref[...] = jnp.zeros_like(carry_ref)
        h_a[...] = h0_ref[...]

    def stage(ext_w, ext_r, h_cur):
        for half in range(2):
            slot = half * nj + je
            prev = jnp.where(seq_start, 0.0, carry_ref[slot])
            carry_ref[slot] = ext_r[half, tm:tm + CARRY_ROWS, :]
            ext_r[half, 0:CARRY_ROWS, :] = prev

        def conv(half, r0, cw_ref, cb_ref):
            slab = ext_r[half, r0:r0 + CARRY_ROWS + EPI_ROWS, :]
            y = cb_ref[...]
            for j in range(CONV_WIDTH - 1, 0, -1):
                tap = pltpu.roll(slab, j, axis=0)[CARRY_ROWS:, :]
                y = y + cw_ref[CONV_WIDTH - 1 - j:CONV_WIDTH - j, :] * tap
            return y + cw_ref[CONV_WIDTH - 1:CONV_WIDTH, :] * slab[CARRY_ROWS:, :]

        kc = h_cur.shape[1] // K_SPLIT
        rows_per_piece = tm // (2 * K_SPLIT)
        piece = 0
        for half, w_ref in enumerate((wg_ref, wv_ref)):
            acc = None
            for p in range(K_SPLIT):
                ks = slice(p * kc, (p + 1) * kc)
                part = jnp.dot(h_cur[:, ks], w_ref[ks, :].astype(BF16), preferred_element_type=F32)
                acc = part if acc is None else acc + part
                for r0 in range(piece * rows_per_piece, (piece + 1) * rows_per_piece, EPI_ROWS):
                    gate = conv(0, r0, cwg_ref, cbg_ref)
                    val = conv(1, r0, cwv_ref, cbv_ref)
                    o_ref[r0:r0 + EPI_ROWS, :] = (jax.nn.silu(gate) * val).astype(o_ref.dtype)
                piece += 1
                if piece == K_SPLIT // 2:
                    wdo_ref[...] = wd_ref[...].astype(wdo_ref.dtype)
            ext_w[half, CARRY_ROWS:, :] = acc
            if half == 0:
                xs = x_ref[...]
                width = xs.shape[1]
                part = None
                for c0 in range(0, width, HEAD_DIM):
                    sq = xs[:, c0:c0 + HEAD_DIM] * xs[:, c0:c0 + HEAD_DIM]
                    part = sq if part is None else part + sq
                hi = part.astype(BF16)
                lo = (part - hi.astype(F32)).astype(BF16)
                ones = jnp.ones((HEAD_DIM, HEAD_DIM), BF16)
                ssq = (jnp.dot(hi, ones, preferred_element_type=F32)
                       + jnp.dot(lo, ones, preferred_element_type=F32))
                rs = lax.rsqrt(ssq * (1.0 / width) + EPS)
                for c0 in range(0, width, HEAD_DIM):
                    cols = slice(c0, c0 + HEAD_DIM)
                    h_stage[:, cols] = (xs[:, cols] * rs * g_ref[:, cols]).astype(h_stage.dtype)

    for tp, (ext_w, ext_r) in enumerate(((ext_a, ext_b), (ext_b, ext_a))):
        for hp, h_cur in enumerate((h_a, h_b)):
            pl.when((t % 2 == tp) & (cur == hp))(functools.partial(stage, ext_w, ext_r, h_cur))

    rows = pl.ds(pl.multiple_of(jn * NORM_ROWS, NORM_ROWS), NORM_ROWS)
    for hp, h_next in enumerate((h_b, h_a)):
        @pl.when(cur == hp)
        def _(h_next=h_next):
            h_next[rows, :] = h_stage[...]


def _upproj(x, g, w, conv_w, conv_b, w_down, layer, seq, tm=1024, tn=256):
    m, k = x.shape
    nj = D_FF // tn
    n_tiles = (m // tm) * nj
    n_down = w_down.shape[2]
    slab = D_FF // n_tiles
    assert slab * n_tiles == D_FF and slab % 16 == 0
    slabs_per_block = tm // NORM_ROWS
    assert slabs_per_block <= nj
    n_slabs = m // NORM_ROWS
    h0 = _rmsnorm(x, g, BF16, rows=tm)
    body = functools.partial(_up_body, tm=tm, nj=nj, n_tiles=n_tiles, blocks_per_seq=seq // tm)
    mm = lambda t: jnp.minimum(t, n_tiles - 1)
    ep = lambda t: jnp.maximum(t - 1, 0)

    def next_slab(t):
        s = (mm(t) // nj + 1) * slabs_per_block + jnp.minimum(mm(t) % nj, slabs_per_block - 1)
        return (jnp.minimum(s, n_slabs - 1), 0)

    return pl.pallas_call(
        body,
        grid=(n_tiles + 1,),
        in_specs=[pl.BlockSpec((NORM_ROWS, k), next_slab),
                  pl.BlockSpec((1, k), lambda t: (0, 0)),
                  pl.BlockSpec((tm, k), lambda t: (0, 0), pipeline_mode=pl.Buffered(1)),
                  pl.BlockSpec((None, k, tn), lambda t: (layer, 0, mm(t) % nj)),
                  pl.BlockSpec((None, k, tn), lambda t: (layer, 0, mm(t) % nj + nj)),
                  pl.BlockSpec((None, CONV_WIDTH, tn), lambda t: (layer, 0, ep(t) % nj)),
                  pl.BlockSpec((None, CONV_WIDTH, tn), lambda t: (layer, 0, ep(t) % nj + nj)),
                  pl.BlockSpec((None, 1, tn), lambda t: (layer, 0, ep(t) % nj)),
                  pl.BlockSpec((None, 1, tn), lambda t: (layer, 0, ep(t) % nj + nj)),
                  pl.BlockSpec((None, slab, n_down), lambda t: (layer, mm(t), 0))],
        out_specs=[pl.BlockSpec((tm, tn), lambda t: (ep(t) // nj, ep(t) % nj)),
                   pl.BlockSpec((slab, n_down), lambda t: (mm(t), 0))],
        out_shape=[jax.ShapeDtypeStruct((m, D_FF), BF16),
                   jax.ShapeDtypeStruct((D_FF, n_down), BF16)],
        scratch_shapes=[pltpu.VMEM((tm, k), BF16),
                        pltpu.VMEM((tm, k), BF16),
                        pltpu.VMEM((NORM_ROWS, k), BF16),
                        pltpu.VMEM((2, tm + CARRY_ROWS, tn), F32),
                        pltpu.VMEM((2, tm + CARRY_ROWS, tn), F32),
                        pltpu.VMEM((2 * nj, CARRY_ROWS, tn), F32)],
        compiler_params=_params("arbitrary"),
        name="upproj_conv_gate",
    )(x, g.reshape(1, k), h0, w, w, conv_w, conv_w, conv_b, conv_b, w_down)


def _down_body(a_ref, w_ref, x_ref, o_ref):
    o_ref[...] = x_ref[...] + jnp.dot(a_ref[...], w_ref[...], preferred_element_type=F32)


def _downproj(a, w, x, tm=512, tn=512):
    m, k = a.shape
    n = w.shape[1]
    return pl.pallas_call(
        _down_body,
        grid=(m // tm, n // tn),
        in_specs=[pl.BlockSpec((tm, k), lambda i, j: (i, 0)),
                  pl.BlockSpec((k, tn), lambda i, j: (0, j)),
                  pl.BlockSpec((tm, tn), lambda i, j: (i, j))],
        out_specs=pl.BlockSpec((tm, tn), lambda i, j: (i, j)),
        out_shape=jax.ShapeDtypeStruct((m, n), F32),
        compiler_params=_params("parallel", "arbitrary"),
        name="downproj",
    )(a, w, x)


def kernel(x, attn_norm_g, w_in, sgu_norm_g, w_spatial, b_spatial, mix_norm_g, w_out, ffn_norm_g,
           w_up, conv_w, conv_b, w_down, final_norm_g):
    b, s, d = x.shape
    depth = w_in.shape[0]
    xm = x.reshape(b * s, d)
    slopes = 2.0 ** (-8.0 * jnp.arange(1, N_HEADS + 1, dtype=F32) / N_HEADS)
    slopes = jnp.broadcast_to(slopes[:, None, None], (N_HEADS, 1, HEAD_DIM))
    conv_b3 = conv_b.reshape(depth, 1, 2 * D_FF)
    for i in range(depth):
        mix_g = mix_norm_g[i].reshape(1, D_MODEL)
        h = _rmsnorm(xm, attn_norm_g[i], BF16)
        z = _inproj(h, w_in, i)
        a = _attention(z.reshape(b, s, N_IN), slopes, mix_g).reshape(b * s, D_ATTN)
        g = _sgu(z, sgu_norm_g[i], w_spatial[i], b_spatial[i].T, mix_g)
        xm = _outproj(a, g, w_out, i, xm)
        act, w_down_bf = _upproj(xm, ffn_norm_g[i], w_up, conv_w, conv_b3, w_down, i, s)
        xm = _downproj(act, w_down_bf, xm)
    return _rmsnorm(xm, final_norm_g, F32).reshape(b, s, d)
```

```python
import functools
import math

import jax
import jax.numpy as jnp
from jax import lax
from jax.experimental import pallas as pl
from jax.experimental.pallas import tpu as pltpu

F32 = jnp.float32
BF16 = jnp.bfloat16

D_MODEL = 4096
HEAD_DIM = 128
D_ATTN = D_MODEL // 2
D_SGU = D_MODEL - D_ATTN
N_HEADS = D_ATTN // HEAD_DIM
N_GROUPS = D_SGU // HEAD_DIM
CHUNK = 128
BLOCK = 128
DILATIONS = (1, 4, 16)
N_IN = 3 * D_ATTN + 2 * D_SGU
D_FF = 256 * ((8 * D_MODEL // 3 + 255) // 256)
CONV_WIDTH = 3
EPS = 1e-6
LOG2E = math.log2(math.e)
VMEM_LIMIT = 61 * 1024 * 1024
CARRY_ROWS = 8
EPI_ROWS = 32
K_SPLIT = 16
NORM_ROWS = 32
SGU_BATCH = 4
ATT_GROUP = 32


def _params(*sem):
    return pltpu.CompilerParams(dimension_semantics=sem, vmem_limit_bytes=VMEM_LIMIT)


def _rmsnorm_body(x_ref, g_ref, o_ref):
    x = x_ref[...]
    y = x * lax.rsqrt(jnp.mean(x * x, axis=-1, keepdims=True) + EPS)
    o_ref[...] = (y * g_ref[...]).astype(o_ref.dtype)


def _rmsnorm(x, g, out_dtype, tm=512, rows=None):
    m, d = x.shape
    m = m if rows is None else rows
    return pl.pallas_call(
        _rmsnorm_body,
        grid=(m // tm,),
        in_specs=[pl.BlockSpec((tm, d), lambda i: (i, 0)),
                  pl.BlockSpec((1, d), lambda i: (0, 0))],
        out_specs=pl.BlockSpec((tm, d), lambda i: (i, 0)),
        out_shape=jax.ShapeDtypeStruct((m, d), out_dtype),
        compiler_params=_params("parallel"),
        name="rmsnorm",
    )(x, g.reshape(1, d))


def _inproj_body(h_ref, w_ref, o_ref):
    w = w_ref[...].astype(BF16)
    o_ref[...] = jnp.dot(h_ref[...], w, preferred_element_type=F32).astype(o_ref.dtype)


def _inproj(h, w, layer, tm=1024, tn=1024):
    m, k = h.shape
    n = w.shape[2]
    return pl.pallas_call(
        _inproj_body,
        grid=(m // tm, n // tn),
        in_specs=[pl.BlockSpec((tm, k), lambda i, j: (i, 0)),
                  pl.BlockSpec((None, k, tn), lambda i, j: (layer, 0, j))],
        out_specs=pl.BlockSpec((tm, tn), lambda i, j: (i, j)),
        out_shape=jax.ShapeDtypeStruct((m, n), BF16),
        compiler_params=_params("parallel", "arbitrary"),
        name="inproj",
    )(h, w)


def _aligned(i, m):
    return i if isinstance(i, int) else pl.multiple_of(i, m)


def _attn_body(q_ref, k_ref, v_ref, slope_ref, g_ref, o_ref, xf, x4f, qd, kd, vd, bias2, ob, mb, lb,
               *, seq):
    qscale = HEAD_DIM ** -0.5 * LOG2E
    d1, d2 = DILATIONS[1], DILATIONS[2]
    assert d2 == d1 * d1 and DILATIONS[0] == 1
    sub1, sub2 = seq // d1, seq // d2
    for src, dst, mul, first_c in ((q_ref, qd, qscale, 0), (k_ref, kd, None, 1), (v_ref, vd, None, 1)):
        if first_c == 0:
            xf[...] = src[...].astype(F32) * mul
            dst[0] = xf[...].astype(BF16)
        else:
            xf[...] = src[...].astype(F32)

        def deint1(r, carry, dst=dst, slot=1 - first_c):
            rows = pl.ds(pl.multiple_of(r * sub1, sub1), sub1)
            x = xf[pl.ds(r, sub1, stride=d1), :]
            x4f[rows, :] = x
            dst[slot, rows, :] = x.astype(BF16)
            return carry

        lax.fori_loop(0, d1, deint1, 0)

        def deint2(r, carry, dst=dst, slot=2 - first_c):
            rows = pl.ds(pl.multiple_of(r * sub2, sub2), sub2)
            x = x4f[pl.ds((r % d1) * sub1 + r // d1, sub2, stride=d1), :]
            dst[slot, rows, :] = x.astype(BF16)
            return carry

        lax.fori_loop(0, d2, deint2, 0)

    slope = slope_ref[...]
    qi = lax.broadcasted_iota(jnp.int32, (BLOCK, 2 * BLOCK), 0)
    kj = lax.broadcasted_iota(jnp.int32, (BLOCK, 2 * BLOCK), 1)
    dist = qi + BLOCK - kj
    valid = (dist >= 0) & (dist <= BLOCK)
    for c, d in enumerate(DILATIONS):
        bias2[c] = jnp.where(valid, (-LOG2E * slope[:, :1]) * (dist * d).astype(F32), -jnp.inf)
    ones = jnp.ones((2 * BLOCK, HEAD_DIM), BF16)

    def group(c, d, blocks):
        getk = (lambda rows: k_ref[rows, :]) if c == 0 else (lambda rows: kd[c - 1, rows, :])
        getv = (lambda rows: v_ref[rows, :]) if c == 0 else (lambda rows: vd[c - 1, rows, :])
        qs, ks, vs = [], [], []
        for gi, _, first in blocks:
            qrows = pl.ds(_aligned(gi * BLOCK, BLOCK), BLOCK)
            krows = qrows if first else pl.ds(_aligned(gi * BLOCK - BLOCK, BLOCK), 2 * BLOCK)
            qs.append(qd[c, qrows, :])
            ks.append(getk(krows))
            v = getv(krows)
            vs.append(jnp.concatenate([v, ones[:v.shape[0]]], axis=1))
        ss = [lax.dot_general(q, k, (((1,), (1,)), ((), ())), preferred_element_type=F32)
              for q, k in zip(qs, ks)]
        ps, ms = [], []
        for s, (_, _, first) in zip(ss, blocks):
            s = s + (bias2[c, :, BLOCK:] if first else bias2[c])
            mx = jnp.max(s, axis=-1, keepdims=True)
            ms.append(mx)
            ps.append(jnp.exp2(s - mx).astype(BF16))
        os = [jnp.dot(p, v, preferred_element_type=F32) for p, v in zip(ps, vs)]
        for o, mx, (_, rows, _) in zip(os, ms, blocks):
            ob[c, rows, :] = o[:, :HEAD_DIM]
            lb[c, rows, :] = o[:, HEAD_DIM:]
            mb[c, rows, :] = jnp.broadcast_to(mx, (BLOCK, HEAD_DIM))

    for c, d in enumerate(DILATIONS):
        nb = seq // d // BLOCK

        def blk(r, n, first, d=d, nb=nb):
            gi = r * nb + n
            if d <= d1:
                return (gi, pl.ds(_aligned(gi * BLOCK, BLOCK), BLOCK), first)
            q = d // d1
            start = (r % d1) * sub1 + q * BLOCK * n + r // d1
            return (gi, pl.ds(start, BLOCK, stride=q), first)

        if nb >= ATT_GROUP:
            def r_body(r, carry, c=c, d=d, nb=nb, blk=blk):
                group(c, d, [blk(r, n, n == 0) for n in range(ATT_GROUP)])

                def n_body(i, carry2):
                    group(c, d, [blk(r, i * ATT_GROUP + t, False) for t in range(ATT_GROUP)])
                    return carry2

                return lax.fori_loop(1, nb // ATT_GROUP, n_body, carry)

            if d == 1:
                r_body(0, 0)
            else:
                lax.fori_loop(0, d, r_body, 0)
        else:
            per = ATT_GROUP // nb

            def r_body(i, carry, c=c, d=d, nb=nb, blk=blk, per=per):
                group(c, d, [blk(i * per + t, n, n == 0) for t in range(per) for n in range(nb)])
                return carry

            lax.fori_loop(0, d // per, r_body, 0)

    g = g_ref[...]
    step, piece = 512, 64

    def fin(t, carry):
        r = (t * step) // sub1
        m0 = (t * step) % sub1
        for p0 in range(0, step, piece):
            rows = pl.ds(pl.multiple_of(t * step + p0, piece), piece)
            toks = pl.ds(r + d1 * (m0 + p0), piece, stride=d1)
            ma, mb1, mb2 = mb[0, toks, :], mb[1, rows, :], mb[2, rows, :]
            mx = jnp.maximum(jnp.maximum(ma, mb1), mb2)
            e0, e1, e2 = jnp.exp2(ma - mx), jnp.exp2(mb1 - mx), jnp.exp2(mb2 - mx)
            den = e0 * lb[0, toks, :] + e1 * lb[1, rows, :] + e2 * lb[2, rows, :]
            num = e0 * ob[0, toks, :] + e1 * ob[1, rows, :] + e2 * ob[2, rows, :]
            msq = jnp.mean(num * num, axis=-1, keepdims=True)
            xf[toks, :] = num * lax.rsqrt(msq + EPS * (den * den)) * g
        return carry

    lax.fori_loop(0, seq // step, fin, 0)
    o_ref[...] = xf[...].astype(o_ref.dtype)


def _attention(z3, slopes, mix_g):
    b, s, _ = z3.shape
    blk = lambda off: pl.BlockSpec((None, s, HEAD_DIM), lambda bi, h, off=off: (bi, 0, off + h))
    return pl.pallas_call(
        functools.partial(_attn_body, seq=s),
        grid=(b, N_HEADS),
        in_specs=[blk(0), blk(N_HEADS), blk(2 * N_HEADS),
                  pl.BlockSpec((None, 1, HEAD_DIM), lambda bi, h: (h, 0, 0)),
                  pl.BlockSpec((1, HEAD_DIM), lambda bi, h: (0, h))],
        out_specs=pl.BlockSpec((None, s, HEAD_DIM), lambda bi, h: (bi, 0, h)),
        out_shape=jax.ShapeDtypeStruct((b, s, D_ATTN), BF16),
        scratch_shapes=[pltpu.VMEM((s, HEAD_DIM), F32)] * 2
        + [pltpu.VMEM((len(DILATIONS), s, HEAD_DIM), BF16)]
        + [pltpu.VMEM((len(DILATIONS) - 1, s, HEAD_DIM), BF16)] * 2
        + [pltpu.VMEM((len(DILATIONS), BLOCK, 2 * BLOCK), F32)]
        + [pltpu.VMEM((len(DILATIONS), s, HEAD_DIM), F32)] * 3,
        compiler_params=_params("parallel", "parallel"),
        name="dilated_attn",
    )(z3, z3, z3, slopes, mix_g)


def _gelu(x):
    return 0.5 * x * (1.0 + lax.erf(x * math.sqrt(0.5)))


def _sgu_body(u_ref, v_ref, ng_ref, w_ref, bt_ref, mg_ref, wo_ref, o_ref, wob_ref, *, tm):
    wob_ref[...] = wo_ref[...].astype(wob_ref.dtype)
    ti = lax.broadcasted_iota(jnp.int32, (CHUNK, CHUNK), 0)
    si = lax.broadcasted_iota(jnp.int32, (CHUNK, CHUNK), 1)
    causal = si <= ti
    n_ch = tm // CHUNK
    for g0 in range(0, N_GROUPS, SGU_BATCH):
        groups = range(g0, g0 + SGU_BATCH)
        vcat = []
        for g in groups:
            cols = slice(g * HEAD_DIM, (g + 1) * HEAD_DIM)
            ng = ng_ref[:, cols]
            vs = []
            for ch in range(n_ch):
                v = _gelu(v_ref[ch * CHUNK:(ch + 1) * CHUNK, cols].astype(F32))
                v = v * lax.rsqrt(jnp.mean(v * v, axis=-1, keepdims=True) + EPS) * ng
                vs.append(v.astype(BF16))
            vcat.append(jnp.concatenate(vs, axis=1))
        mixed = [jnp.dot(jnp.where(causal, w_ref[g], 0.0).astype(BF16), vc, preferred_element_type=F32)
                 for g, vc in zip(groups, vcat)]
        for g, mx in zip(groups, mixed):
            cols = slice(g * HEAD_DIM, (g + 1) * HEAD_DIM)
            bcol = bt_ref[:, g:g + 1]
            mg = mg_ref[:, cols]
            for ch in range(n_ch):
                rows = slice(ch * CHUNK, (ch + 1) * CHUNK)
                u = _gelu(u_ref[rows, cols].astype(F32))
                y = u * (mx[:, ch * HEAD_DIM:(ch + 1) * HEAD_DIM] + bcol)
                y = y * lax.rsqrt(jnp.mean(y * y, axis=-1, keepdims=True) + EPS) * mg
                o_ref[rows, cols] = y.astype(o_ref.dtype)


def _sgu(z, sgu_norm_g, w_spatial, b_spatial_t, mix_g, w_out, layer, tm=256):
    m = z.shape[0]
    steps = m // tm
    k_out, n_out = w_out.shape[1:]
    slab = k_out // steps
    assert slab * steps == k_out and slab % 16 == 0
    ublk = 3 * D_ATTN // D_SGU
    return pl.pallas_call(
        functools.partial(_sgu_body, tm=tm),
        grid=(steps,),
        in_specs=[pl.BlockSpec((tm, D_SGU), lambda i: (i, ublk)),
                  pl.BlockSpec((tm, D_SGU), lambda i: (i, ublk + 1)),
                  pl.BlockSpec((1, D_SGU), lambda i: (0, 0)),
                  pl.BlockSpec((N_GROUPS, CHUNK, CHUNK), lambda i: (0, 0, 0)),
                  pl.BlockSpec((CHUNK, N_GROUPS), lambda i: (0, 0)),
                  pl.BlockSpec((1, D_SGU), lambda i: (0, 1)),
                  pl.BlockSpec((None, slab, n_out), lambda i: (layer, i, 0))],
        out_specs=[pl.BlockSpec((tm, D_SGU), lambda i: (i, 0)),
                   pl.BlockSpec((slab, n_out), lambda i: (i, 0))],
        out_shape=[jax.ShapeDtypeStruct((m, D_SGU), BF16),
                   jax.ShapeDtypeStruct((k_out, n_out), BF16)],
        compiler_params=_params("parallel"),
        name="sgu",
    )(z, z, sgu_norm_g.reshape(1, D_SGU), w_spatial, b_spatial_t, mix_g, w_out)


def _outproj_body(a_ref, g_ref, wa_ref, wg_ref, x_ref, o_ref):
    acc = jnp.dot(a_ref[...], wa_ref[...], preferred_element_type=F32)
    acc = acc + jnp.dot(g_ref[...], wg_ref[...], preferred_element_type=F32)
    o_ref[...] = x_ref[...] + acc


def _outproj(a, g, w, x, tm=1024, tn=1024):
    m, ka = a.shape
    kg = g.shape[1]
    assert ka == kg
    n = w.shape[1]
    return pl.pallas_call(
        _outproj_body,
        grid=(m // tm, n // tn),
        in_specs=[pl.BlockSpec((tm, ka), lambda i, j: (i, 0)),
                  pl.BlockSpec((tm, kg), lambda i, j: (i, 0)),
                  pl.BlockSpec((ka, tn), lambda i, j: (0, j)),
                  pl.BlockSpec((kg, tn), lambda i, j: (1, j)),
                  pl.BlockSpec((tm, tn), lambda i, j: (i, j))],
        out_specs=pl.BlockSpec((tm, tn), lambda i, j: (i, j)),
        out_shape=jax.ShapeDtypeStruct((m, n), F32),
        compiler_params=_params("parallel", "arbitrary"),
        name="outproj",
    )(a, g, w, w, x)


def _up_body(x_ref, g_ref, h0_ref, wg_ref, wv_ref, cwg_ref, cwv_ref, cbg_ref, cbv_ref, wd_ref,
             o_ref, wdo_ref, h_a, h_b, h_stage, ext_a, ext_b, carry_ref, *, tm, nj, n_tiles, blocks_per_seq):
    t = pl.program_id(0)
    te = jnp.maximum(t - 1, 0)
    je = te % nj
    seq_start = ((te // nj) % blocks_per_seq) == 0
    tmm = jnp.minimum(t, n_tiles - 1)
    cur = (tmm // nj) % 2
    jn = jnp.minimum(tmm % nj, tm // NORM_ROWS - 1)

    @pl.when(t == 0)
    def _():
        ext_a[...] = jnp.zeros_like(ext_a)
        ext_b[...] = jnp.zeros_like(ext_b)
        carry_ref[...] = jnp.zeros_like(carry_ref)
        h_a[...] = h0_ref[...]

    def stage(ext_w, ext_r, h_cur):
        for half in range(2):
            slot = half * nj + je
            prev = jnp.where(seq_start, 0.0, carry_ref[slot])
            carry_ref[slot] = ext_r[half, tm:tm + CARRY_ROWS, :]
            ext_r[half, 0:CARRY_ROWS, :] = prev

        def conv(half, r0, cw_ref, cb_ref):
            slab = ext_r[half, r0:r0 + CARRY_ROWS + EPI_ROWS, :]
            y = cb_ref[...]
            for j in range(CONV_WIDTH - 1, 0, -1):
                tap = pltpu.roll(slab, j, axis=0)[CARRY_ROWS:, :]
                y = y + cw_ref[CONV_WIDTH - 1 - j:CONV_WIDTH - j, :] * tap
            return y + cw_ref[CONV_WIDTH - 1:CONV_WIDTH, :] * slab[CARRY_ROWS:, :]

        kc = h_cur.shape[1] // K_SPLIT
        rows_per_piece = tm // (2 * K_SPLIT)
        piece = 0
        for half, w_ref in enumerate((wg_ref, wv_ref)):
            acc = None
            for p in range(K_SPLIT):
                ks = slice(p * kc, (p + 1) * kc)
                part = jnp.dot(h_cur[:, ks], w_ref[ks, :].astype(BF16), preferred_element_type=F32)
                acc = part if acc is None else acc + part
                for r0 in range(piece * rows_per_piece, (piece + 1) * rows_per_piece, EPI_ROWS):
                    gate = conv(0, r0, cwg_ref, cbg_ref)
                    val = conv(1, r0, cwv_ref, cbv_ref)
                    o_ref[r0:r0 + EPI_ROWS, :] = (jax.nn.silu(gate) * val).astype(o_ref.dtype)
                piece += 1
                if piece == K_SPLIT // 2:
                    wdo_ref[...] = wd_ref[...].astype(wdo_ref.dtype)
            ext_w[half, CARRY_ROWS:, :] = acc
            if half == 0:
                xs = x_ref[...]
                width = xs.shape[1]
                part = None
                for c0 in range(0, width, HEAD_DIM):
                    sq = xs[:, c0:c0 + HEAD_DIM] * xs[:, c0:c0 + HEAD_DIM]
                    part = sq if part is None else part + sq
                hi = part.astype(BF16)
                lo = (part - hi.astype(F32)).astype(BF16)
                ones = jnp.ones((HEAD_DIM, HEAD_DIM), BF16)
                ssq = (jnp.dot(hi, ones, preferred_element_type=F32)
                       + jnp.dot(lo, ones, preferred_element_type=F32))
                rs = lax.rsqrt(ssq * (1.0 / width) + EPS)
                for c0 in range(0, width, HEAD_DIM):
                    cols = slice(c0, c0 + HEAD_DIM)
                    h_stage[:, cols] = (xs[:, cols] * rs * g_ref[:, cols]).astype(h_stage.dtype)

    for tp, (ext_w, ext_r) in enumerate(((ext_a, ext_b), (ext_b, ext_a))):
        for hp, h_cur in enumerate((h_a, h_b)):
            pl.when((t % 2 == tp) & (cur == hp))(functools.partial(stage, ext_w, ext_r, h_cur))

    rows = pl.ds(pl.multiple_of(jn * NORM_ROWS, NORM_ROWS), NORM_ROWS)
    for hp, h_next in enumerate((h_b, h_a)):
        @pl.when(cur == hp)
        def _(h_next=h_next):
            h_next[rows, :] = h_stage[...]


def _upproj(x, g, w, conv_w, conv_b, w_down, layer, seq, tm=1024, tn=256):
    m, k = x.shape
    nj = D_FF // tn
    n_tiles = (m // tm) * nj
    n_down = w_down.shape[2]
    slab = D_FF // n_tiles
    assert slab * n_tiles == D_FF and slab % 16 == 0
    slabs_per_block = tm // NORM_ROWS
    assert slabs_per_block <= nj
    n_slabs = m // NORM_ROWS
    h0 = _rmsnorm(x, g, BF16, rows=tm)
    body = functools.partial(_up_body, tm=tm, nj=nj, n_tiles=n_tiles, blocks_per_seq=seq // tm)
    mm = lambda t: jnp.minimum(t, n_tiles - 1)
    ep = lambda t: jnp.maximum(t - 1, 0)

    def next_slab(t):
        s = (mm(t) // nj + 1) * slabs_per_block + jnp.minimum(mm(t) % nj, slabs_per_block - 1)
        return (jnp.minimum(s, n_slabs - 1), 0)

    return pl.pallas_call(
        body,
        grid=(n_tiles + 1,),
        in_specs=[pl.BlockSpec((NORM_ROWS, k), next_slab),
                  pl.BlockSpec((1, k), lambda t: (0, 0)),
                  pl.BlockSpec((tm, k), lambda t: (0, 0), pipeline_mode=pl.Buffered(1)),
                  pl.BlockSpec((None, k, tn), lambda t: (layer, 0, mm(t) % nj)),
                  pl.BlockSpec((None, k, tn), lambda t: (layer, 0, mm(t) % nj + nj)),
                  pl.BlockSpec((None, CONV_WIDTH, tn), lambda t: (layer, 0, ep(t) % nj)),
                  pl.BlockSpec((None, CONV_WIDTH, tn), lambda t: (layer, 0, ep(t) % nj + nj)),
                  pl.BlockSpec((None, 1, tn), lambda t: (layer, 0, ep(t) % nj)),
                  pl.BlockSpec((None, 1, tn), lambda t: (layer, 0, ep(t) % nj + nj)),
                  pl.BlockSpec((None, slab, n_down), lambda t: (layer, mm(t), 0))],
        out_specs=[pl.BlockSpec((tm, tn), lambda t: (ep(t) // nj, ep(t) % nj)),
                   pl.BlockSpec((slab, n_down), lambda t: (mm(t), 0))],
        out_shape=[jax.ShapeDtypeStruct((m, D_FF), BF16),
                   jax.ShapeDtypeStruct((D_FF, n_down), BF16)],
        scratch_shapes=[pltpu.VMEM((tm, k), BF16),
                        pltpu.VMEM((tm, k), BF16),
                        pltpu.VMEM((NORM_ROWS, k), BF16),
                        pltpu.VMEM((2, tm + CARRY_ROWS, tn), F32),
                        pltpu.VMEM((2, tm + CARRY_ROWS, tn), F32),
                        pltpu.VMEM((2 * nj, CARRY_ROWS, tn), F32)],
        compiler_params=_params("arbitrary"),
        name="upproj_conv_gate",
    )(x, g.reshape(1, k), h0, w, w, conv_w, conv_w, conv_b, conv_b, w_down)


def _down_body(a_ref, w_ref, x_ref, o_ref):
    o_ref[...] = x_ref[...] + jnp.dot(a_ref[...], w_ref[...], preferred_element_type=F32)


def _downproj(a, w, x, tm=512, tn=512):
    m, k = a.shape
    n = w.shape[1]
    return pl.pallas_call(
        _down_body,
        grid=(m // tm, n // tn),
        in_specs=[pl.BlockSpec((tm, k), lambda i, j: (i, 0)),
                  pl.BlockSpec((k, tn), lambda i, j: (0, j)),
                  pl.BlockSpec((tm, tn), lambda i, j: (i, j))],
        out_specs=pl.BlockSpec((tm, tn), lambda i, j: (i, j)),
        out_shape=jax.ShapeDtypeStruct((m, n), F32),
        compiler_params=_params("parallel", "arbitrary"),
        name="downproj",
    )(a, w, x)


def kernel(x, attn_norm_g, w_in, sgu_norm_g, w_spatial, b_spatial, mix_norm_g, w_out, ffn_norm_g,
           w_up, conv_w, conv_b, w_down, final_norm_g):
    b, s, d = x.shape
    depth = w_in.shape[0]
    xm = x.reshape(b * s, d)
    slopes = 2.0 ** (-8.0 * jnp.arange(1, N_HEADS + 1, dtype=F32) / N_HEADS)
    slopes = jnp.broadcast_to(slopes[:, None, None], (N_HEADS, 1, HEAD_DIM))
    conv_b3 = conv_b.reshape(depth, 1, 2 * D_FF)
    for i in range(depth):
        mix_g = mix_norm_g[i].reshape(1, D_MODEL)
        h = _rmsnorm(xm, attn_norm_g[i], BF16)
        z = _inproj(h, w_in, i)
        a = _attention(z.reshape(b, s, N_IN), slopes, mix_g).reshape(b * s, D_ATTN)
        g, w_out_bf = _sgu(z, sgu_norm_g[i], w_spatial[i], b_spatial[i].T, mix_g, w_out, i)
        xm = _outproj(a, g, w_out_bf, xm)
        act, w_down_bf = _upproj(xm, ffn_norm_g[i], w_up, conv_w, conv_b3, w_down, i, s)
        xm = _downproj(act, w_down_bf, xm)
    return _rmsnorm(xm, final_norm_g, F32).reshape(b, s, d)
```

```python
import functools
import math

import jax
import jax.numpy as jnp
from jax import lax
from jax.experimental import pallas as pl
from jax.experimental.pallas import tpu as pltpu

F32 = jnp.float32
BF16 = jnp.bfloat16

D_MODEL = 4096
HEAD_DIM = 128
D_ATTN = D_MODEL // 2
D_SGU = D_MODEL - D_ATTN
N_HEADS = D_ATTN // HEAD_DIM
N_GROUPS = D_SGU // HEAD_DIM
CHUNK = 128
BLOCK = 128
DILATIONS = (1, 4, 16)
N_IN = 3 * D_ATTN + 2 * D_SGU
D_FF = 256 * ((8 * D_MODEL // 3 + 255) // 256)
CONV_WIDTH = 3
EPS = 1e-6
LOG2E = math.log2(math.e)
VMEM_LIMIT = 61 * 1024 * 1024
CARRY_ROWS = 8
EPI_ROWS = 32
K_SPLIT = 16
NORM_ROWS = 32
SGU_BATCH = 4
ATT_GROUP = 32


def _params(*sem):
    return pltpu.CompilerParams(dimension_semantics=sem, vmem_limit_bytes=VMEM_LIMIT)


def _rmsnorm_body(x_ref, g_ref, o_ref):
    x = x_ref[...]
    y = x * lax.rsqrt(jnp.mean(x * x, axis=-1, keepdims=True) + EPS)
    o_ref[...] = (y * g_ref[...]).astype(o_ref.dtype)


def _rmsnorm(x, g, out_dtype, tm=512, rows=None):
    m, d = x.shape
    m = m if rows is None else rows
    return pl.pallas_call(
        _rmsnorm_body,
        grid=(m // tm,),
        in_specs=[pl.BlockSpec((tm, d), lambda i: (i, 0)),
                  pl.BlockSpec((1, d), lambda i: (0, 0))],
        out_specs=pl.BlockSpec((tm, d), lambda i: (i, 0)),
        out_shape=jax.ShapeDtypeStruct((m, d), out_dtype),
        compiler_params=_params("parallel"),
        name="rmsnorm",
    )(x, g.reshape(1, d))


def _inproj_body(h_ref, w_ref, o_ref):
    w = w_ref[...].astype(BF16)
    o_ref[...] = jnp.dot(h_ref[...], w, preferred_element_type=F32).astype(o_ref.dtype)


def _inproj(h, w, layer, tm=1024, tn=1024):
    m, k = h.shape
    n = w.shape[2]
    return pl.pallas_call(
        _inproj_body,
        grid=(m // tm, n // tn),
        in_specs=[pl.BlockSpec((tm, k), lambda i, j: (i, 0)),
                  pl.BlockSpec((None, k, tn), lambda i, j: (layer, 0, j))],
        out_specs=pl.BlockSpec((tm, tn), lambda i, j: (i, j)),
        out_shape=jax.ShapeDtypeStruct((m, n), BF16),
        compiler_params=_params("parallel", "arbitrary"),
        name="inproj",
    )(h, w)


def _aligned(i, m):
    return i if isinstance(i, int) else pl.multiple_of(i, m)


def _attn_body(q_ref, k_ref, v_ref, slope_ref, g_ref, o_ref, xf, x4f, qd, kd, vd, bias2, ob, mb, lb,
               *, seq):
    qscale = HEAD_DIM ** -0.5 * LOG2E
    d1, d2 = DILATIONS[1], DILATIONS[2]
    assert d2 == d1 * d1 and DILATIONS[0] == 1
    sub1, sub2 = seq // d1, seq // d2
    for src, dst, mul, first_c in ((q_ref, qd, qscale, 0), (k_ref, kd, None, 1), (v_ref, vd, None, 1)):
        if first_c == 0:
            xf[...] = src[...].astype(F32) * mul
            dst[0] = xf[...].astype(BF16)
        else:
            xf[...] = src[...].astype(F32)

        def deint1(r, carry, dst=dst, slot=1 - first_c):
            rows = pl.ds(pl.multiple_of(r * sub1, sub1), sub1)
            x = xf[pl.ds(r, sub1, stride=d1), :]
            x4f[rows, :] = x
            dst[slot, rows, :] = x.astype(BF16)
            return carry

        lax.fori_loop(0, d1, deint1, 0)

        def deint2(r, carry, dst=dst, slot=2 - first_c):
            rows = pl.ds(pl.multiple_of(r * sub2, sub2), sub2)
            x = x4f[pl.ds((r % d1) * sub1 + r // d1, sub2, stride=d1), :]
            dst[slot, rows, :] = x.astype(BF16)
            return carry

        lax.fori_loop(0, d2, deint2, 0)

    slope = slope_ref[...]
    qi = lax.broadcasted_iota(jnp.int32, (BLOCK, 2 * BLOCK), 0)
    kj = lax.broadcasted_iota(jnp.int32, (BLOCK, 2 * BLOCK), 1)
    dist = qi + BLOCK - kj
    valid = (dist >= 0) & (dist <= BLOCK)
    for c, d in enumerate(DILATIONS):
        bias2[c] = jnp.where(valid, (-LOG2E * slope[:, :1]) * (dist * d).astype(F32), -jnp.inf)
    ones = jnp.ones((2 * BLOCK, HEAD_DIM), BF16)

    def group(c, d, blocks):
        getk = (lambda rows: k_ref[rows, :]) if c == 0 else (lambda rows: kd[c - 1, rows, :])
        getv = (lambda rows: v_ref[rows, :]) if c == 0 else (lambda rows: vd[c - 1, rows, :])
        qs, ks, vs = [], [], []
        for gi, _, first in blocks:
            qrows = pl.ds(_aligned(gi * BLOCK, BLOCK), BLOCK)
            krows = qrows if first else pl.ds(_aligned(gi * BLOCK - BLOCK, BLOCK), 2 * BLOCK)
            qs.append(qd[c, qrows, :])
            ks.append(getk(krows))
            v = getv(krows)
            vs.append(jnp.concatenate([v, ones[:v.shape[0]]], axis=1))
        ss = [lax.dot_general(q, k, (((1,), (1,)), ((), ())), preferred_element_type=F32)
              for q, k in zip(qs, ks)]
        ps, ms = [], []
        for s, (_, _, first) in zip(ss, blocks):
            s = s + (bias2[c, :, BLOCK:] if first else bias2[c])
            mx = jnp.max(s, axis=-1, keepdims=True)
            ms.append(mx)
            ps.append(jnp.exp2(s - mx).astype(BF16))
        os = [jnp.dot(p, v, preferred_element_type=F32) for p, v in zip(ps, vs)]
        for o, mx, (_, rows, _) in zip(os, ms, blocks):
            ob[c, rows, :] = o[:, :HEAD_DIM]
            lb[c, rows, :] = o[:, HEAD_DIM:]
            mb[c, rows, :] = jnp.broadcast_to(mx, (BLOCK, HEAD_DIM))

    for c, d in enumerate(DILATIONS):
        nb = seq // d // BLOCK

        def blk(r, n, first, d=d, nb=nb):
            gi = r * nb + n
            if d <= d1:
                return (gi, pl.ds(_aligned(gi * BLOCK, BLOCK), BLOCK), first)
            q = d // d1
            start = (r % d1) * sub1 + q * BLOCK * n + r // d1
            return (gi, pl.ds(start, BLOCK, stride=q), first)

        if nb >= ATT_GROUP:
            def r_body(r, carry, c=c, d=d, nb=nb, blk=blk):
                group(c, d, [blk(r, n, n == 0) for n in range(ATT_GROUP)])

                def n_body(i, carry2):
                    group(c, d, [blk(r, i * ATT_GROUP + t, False) for t in range(ATT_GROUP)])
                    return carry2

                return lax.fori_loop(1, nb // ATT_GROUP, n_body, carry)

            if d == 1:
                r_body(0, 0)
            else:
                lax.fori_loop(0, d, r_body, 0)
        else:
            per = ATT_GROUP // nb

            def r_body(i, carry, c=c, d=d, nb=nb, blk=blk, per=per):
                group(c, d, [blk(i * per + t, n, n == 0) for t in range(per) for n in range(nb)])
                return carry

            lax.fori_loop(0, d // per, r_body, 0)

    g = g_ref[...]
    step, piece = 512, 64

    def fin(t, carry):
        r = (t * step) // sub1
        m0 = (t * step) % sub1
        for p0 in range(0, step, piece):
            rows = pl.ds(pl.multiple_of(t * step + p0, piece), piece)
            toks = pl.ds(r + d1 * (m0 + p0), piece, stride=d1)
            ma, mb1, mb2 = mb[0, toks, :], mb[1, rows, :], mb[2, rows, :]
            mx = jnp.maximum(jnp.maximum(ma, mb1), mb2)
            e0, e1, e2 = jnp.exp2(ma - mx), jnp.exp2(mb1 - mx), jnp.exp2(mb2 - mx)
            den = e0 * lb[0, toks, :] + e1 * lb[1, rows, :] + e2 * lb[2, rows, :]
            num = e0 * ob[0, toks, :] + e1 * ob[1, rows, :] + e2 * ob[2, rows, :]
            msq = jnp.mean(num * num, axis=-1, keepdims=True)
            xf[toks, :] = num * lax.rsqrt(msq + EPS * (den * den)) * g
        return carry

    lax.fori_loop(0, seq // step, fin, 0)
    o_ref[...] = xf[...].astype(o_ref.dtype)


def _attention(z3, slopes, mix_g):
    b, s, _ = z3.shape
    blk = lambda off: pl.BlockSpec((None, s, HEAD_DIM), lambda bi, h, off=off: (bi, 0, off + h))
    return pl.pallas_call(
        functools.partial(_attn_body, seq=s),
        grid=(b, N_HEADS),
        in_specs=[blk(0), blk(N_HEADS), blk(2 * N_HEADS),
                  pl.BlockSpec((None, 1, HEAD_DIM), lambda bi, h: (h, 0, 0)),
                  pl.BlockSpec((1, HEAD_DIM), lambda bi, h: (0, h))],
        out_specs=pl.BlockSpec((None, s, HEAD_DIM), lambda bi, h: (bi, 0, h)),
        out_shape=jax.ShapeDtypeStruct((b, s, D_ATTN), BF16),
        scratch_shapes=[pltpu.VMEM((s, HEAD_DIM), F32)] * 2
        + [pltpu.VMEM((len(DILATIONS), s, HEAD_DIM), BF16)]
        + [pltpu.VMEM((len(DILATIONS) - 1, s, HEAD_DIM), BF16)] * 2
        + [pltpu.VMEM((len(DILATIONS), BLOCK, 2 * BLOCK), F32)]
        + [pltpu.VMEM((len(DILATIONS), s, HEAD_DIM), F32)] * 3,
        compiler_params=_params("parallel", "parallel"),
        name="dilated_attn",
    )(z3, z3, z3, slopes, mix_g)


def _gelu(x):
    return 0.5 * x * (1.0 + lax.erf(x * math.sqrt(0.5)))


def _sgu_body(u_ref, v_ref, ng_ref, w_ref, bt_ref, mg_ref, wo_ref, o_ref, wob_ref, *, tm):
    wob_ref[...] = wo_ref[...].astype(wob_ref.dtype)
    ti = lax.broadcasted_iota(jnp.int32, (CHUNK, CHUNK), 0)
    si = lax.broadcasted_iota(jnp.int32, (CHUNK, CHUNK), 1)
    causal = si <= ti
    n_ch = tm // CHUNK
    for g0 in range(0, N_GROUPS, SGU_BATCH):
        groups = range(g0, g0 + SGU_BATCH)
        vcat = []
        for g in groups:
            cols = slice(g * HEAD_DIM, (g + 1) * HEAD_DIM)
            ng = ng_ref[:, cols]
            vs = []
            for ch in range(n_ch):
                v = _gelu(v_ref[ch * CHUNK:(ch + 1) * CHUNK, cols].astype(F32))
                v = v * lax.rsqrt(jnp.mean(v * v, axis=-1, keepdims=True) + EPS) * ng
                vs.append(v.astype(BF16))
            vcat.append(jnp.concatenate(vs, axis=1))
        mixed = [jnp.dot(jnp.where(causal, w_ref[g], 0.0).astype(BF16), vc, preferred_element_type=F32)
                 for g, vc in zip(groups, vcat)]
        for g, mx in zip(groups, mixed):
            cols = slice(g * HEAD_DIM, (g + 1) * HEAD_DIM)
            bcol = bt_ref[:, g:g + 1]
            mg = mg_ref[:, cols]
            for ch in range(n_ch):
                rows = slice(ch * CHUNK, (ch + 1) * CHUNK)
                u = _gelu(u_ref[rows, cols].astype(F32))
                y = u * (mx[:, ch * HEAD_DIM:(ch + 1) * HEAD_DIM] + bcol)
                y = y * lax.rsqrt(jnp.mean(y * y, axis=-1, keepdims=True) + EPS) * mg
                o_ref[rows, cols] = y.astype(o_ref.dtype)


def _sgu(z, sgu_norm_g, w_spatial, b_spatial_t, mix_g, w_out, layer, tm=256):
    m = z.shape[0]
    steps = m // tm
    k_out, n_out = w_out.shape[1:]
    slab = k_out // steps
    assert slab * steps == k_out and slab % 16 == 0
    ublk = 3 * D_ATTN // D_SGU
    return pl.pallas_call(
        functools.partial(_sgu_body, tm=tm),
        grid=(steps,),
        in_specs=[pl.BlockSpec((tm, D_SGU), lambda i: (i, ublk)),
                  pl.BlockSpec((tm, D_SGU), lambda i: (i, ublk + 1)),
                  pl.BlockSpec((1, D_SGU), lambda i: (0, 0)),
                  pl.BlockSpec((N_GROUPS, CHUNK, CHUNK), lambda i: (0, 0, 0)),
                  pl.BlockSpec((CHUNK, N_GROUPS), lambda i: (0, 0)),
                  pl.BlockSpec((1, D_SGU), lambda i: (0, 1)),
                  pl.BlockSpec((None, slab, n_out), lambda i: (layer, i, 0))],
        out_specs=[pl.BlockSpec((tm, D_SGU), lambda i: (i, 0)),
                   pl.BlockSpec((slab, n_out), lambda i: (i, 0))],
        out_shape=[jax.ShapeDtypeStruct((m, D_SGU), BF16),
                   jax.ShapeDtypeStruct((k_out, n_out), BF16)],
        compiler_params=_params("parallel"),
        name="sgu",
    )(z, z, sgu_norm_g.reshape(1, D_SGU), w_spatial, b_spatial_t, mix_g, w_out)


def _outproj_body(a_ref, g_ref, wa_ref, wg_ref, x_ref, o_ref):
    acc = jnp.dot(a_ref[...], wa_ref[...], preferred_element_type=F32)
    acc = acc + jnp.dot(g_ref[...], wg_ref[...], preferred_element_type=F32)
    o_ref[...] = x_ref[...] + acc


def _outproj(a, g, w, x, tm=1024, tn=1024):
    m, ka = a.shape
    kg = g.shape[1]
    assert ka == kg
    n = w.shape[1]
    return pl.pallas_call(
        _outproj_body,
        grid=(m // tm, n // tn),
        in_specs=[pl.BlockSpec((tm, ka), lambda i, j: (i, 0)),
                  pl.BlockSpec((tm, kg), lambda i, j: (i, 0)),
                  pl.BlockSpec((ka, tn), lambda i, j: (0, j)),
                  pl.BlockSpec((kg, tn), lambda i, j: (1, j)),
                  pl.BlockSpec((tm, tn), lambda i, j: (i, j))],
        out_specs=pl.BlockSpec((tm, tn), lambda i, j: (i, j)),
        out_shape=jax.ShapeDtypeStruct((m, n), F32),
        compiler_params=_params("parallel", "arbitrary"),
        name="outproj",
    )(a, g, w, w, x)


def _up_body(x_ref, g_ref, h0_ref, wg_ref, wv_ref, cpg_ref, cpv_ref, wd_ref,
             o_ref, wdo_ref, h_a, h_b, h_stage, ext_a, ext_b, carry_ref, *, tm, nj, n_tiles, blocks_per_seq):
    t = pl.program_id(0)
    te = jnp.maximum(t - 1, 0)
    je = te % nj
    seq_start = ((te // nj) % blocks_per_seq) == 0
    tmm = jnp.minimum(t, n_tiles - 1)
    cur = (tmm // nj) % 2
    jn = jnp.minimum(tmm % nj, tm // NORM_ROWS - 1)

    @pl.when(t == 0)
    def _():
        ext_a[...] = jnp.zeros_like(ext_a)
        ext_b[...] = jnp.zeros_like(ext_b)
        carry_ref[...] = jnp.zeros_like(carry_ref)
        h_a[...] = h0_ref[...]

    def stage(ext_w, ext_r, h_cur):
        for half in range(2):
            slot = half * nj + je
            prev = jnp.where(seq_start, 0.0, carry_ref[slot])
            carry_ref[slot] = ext_r[half, tm:tm + CARRY_ROWS, :]
            ext_r[half, 0:CARRY_ROWS, :] = prev

        def conv(half, r0, cp_ref):
            slab = ext_r[half, r0:r0 + CARRY_ROWS + EPI_ROWS, :]
            y = cp_ref[CONV_WIDTH:CONV_WIDTH + 1, :]
            for j in range(CONV_WIDTH - 1, 0, -1):
                tap = pltpu.roll(slab, j, axis=0)[CARRY_ROWS:, :]
                y = y + cp_ref[CONV_WIDTH - 1 - j:CONV_WIDTH - j, :] * tap
            return y + cp_ref[CONV_WIDTH - 1:CONV_WIDTH, :] * slab[CARRY_ROWS:, :]

        kc = h_cur.shape[1] // K_SPLIT
        rows_per_piece = tm // (2 * K_SPLIT)
        piece = 0
        for half, w_ref in enumerate((wg_ref, wv_ref)):
            acc = None
            for p in range(K_SPLIT):
                ks = slice(p * kc, (p + 1) * kc)
                part = jnp.dot(h_cur[:, ks], w_ref[ks, :].astype(BF16), preferred_element_type=F32)
                acc = part if acc is None else acc + part
                for r0 in range(piece * rows_per_piece, (piece + 1) * rows_per_piece, EPI_ROWS):
                    gate = conv(0, r0, cpg_ref)
                    val = conv(1, r0, cpv_ref)
                    o_ref[r0:r0 + EPI_ROWS, :] = (jax.nn.silu(gate) * val).astype(o_ref.dtype)
                piece += 1
                if piece == K_SPLIT // 2:
                    wdo_ref[...] = wd_ref[...].astype(wdo_ref.dtype)
            ext_w[half, CARRY_ROWS:, :] = acc
            if half == 0:
                xs = x_ref[...]
                width = xs.shape[1]
                part = None
                for c0 in range(0, width, HEAD_DIM):
                    sq = xs[:, c0:c0 + HEAD_DIM] * xs[:, c0:c0 + HEAD_DIM]
                    part = sq if part is None else part + sq
                hi = part.astype(BF16)
                lo = (part - hi.astype(F32)).astype(BF16)
                ones = jnp.ones((HEAD_DIM, HEAD_DIM), BF16)
                ssq = (jnp.dot(hi, ones, preferred_element_type=F32)
                       + jnp.dot(lo, ones, preferred_element_type=F32))
                rs = lax.rsqrt(ssq * (1.0 / width) + EPS)
                for c0 in range(0, width, HEAD_DIM):
                    cols = slice(c0, c0 + HEAD_DIM)
                    h_stage[:, cols] = (xs[:, cols] * rs * g_ref[:, cols]).astype(h_stage.dtype)

    for tp, (ext_w, ext_r) in enumerate(((ext_a, ext_b), (ext_b, ext_a))):
        for hp, h_cur in enumerate((h_a, h_b)):
            pl.when((t % 2 == tp) & (cur == hp))(functools.partial(stage, ext_w, ext_r, h_cur))

    rows = pl.ds(pl.multiple_of(jn * NORM_ROWS, NORM_ROWS), NORM_ROWS)
    fresh = tmm % nj < tm // NORM_ROWS
    for hp, h_next in enumerate((h_b, h_a)):
        @pl.when((cur == hp) & fresh)
        def _(h_next=h_next):
            h_next[rows, :] = h_stage[...]


def _upproj(x, g, w, conv_p, w_down, layer, seq, tm=1024, tn=256):
    m, k = x.shape
    assert seq % tm == 0 and m % seq == 0 and D_FF % tn == 0 and k % (K_SPLIT * 256) == 0
    nj = D_FF // tn
    n_tiles = (m // tm) * nj
    n_down = w_down.shape[2]
    slab = D_FF // n_tiles
    assert slab * n_tiles == D_FF and slab % 16 == 0
    slabs_per_block = tm // NORM_ROWS
    assert slabs_per_block <= nj
    n_slabs = m // NORM_ROWS
    h0 = _rmsnorm(x, g, BF16, rows=tm)
    body = functools.partial(_up_body, tm=tm, nj=nj, n_tiles=n_tiles, blocks_per_seq=seq // tm)
    mm = lambda t: jnp.minimum(t, n_tiles - 1)
    ep = lambda t: jnp.maximum(t - 1, 0)

    def next_slab(t):
        s = (mm(t) // nj + 1) * slabs_per_block + jnp.minimum(mm(t) % nj, slabs_per_block - 1)
        return (jnp.minimum(s, n_slabs - 1), 0)

    return pl.pallas_call(
        body,
        grid=(n_tiles + 1,),
        in_specs=[pl.BlockSpec((NORM_ROWS, k), next_slab),
                  pl.BlockSpec((1, k), lambda t: (0, 0)),
                  pl.BlockSpec((tm, k), lambda t: (0, 0), pipeline_mode=pl.Buffered(1)),
                  pl.BlockSpec((None, k, tn), lambda t: (layer, 0, mm(t) % nj)),
                  pl.BlockSpec((None, k, tn), lambda t: (layer, 0, mm(t) % nj + nj)),
                  pl.BlockSpec((None, CONV_WIDTH + 1, tn), lambda t: (layer, 0, ep(t) % nj)),
                  pl.BlockSpec((None, CONV_WIDTH + 1, tn), lambda t: (layer, 0, ep(t) % nj + nj)),
                  pl.BlockSpec((None, slab, n_down), lambda t: (layer, mm(t), 0))],
        out_specs=[pl.BlockSpec((tm, tn), lambda t: (ep(t) // nj, ep(t) % nj)),
                   pl.BlockSpec((slab, n_down), lambda t: (mm(t), 0))],
        out_shape=[jax.ShapeDtypeStruct((m, D_FF), BF16),
                   jax.ShapeDtypeStruct((D_FF, n_down), BF16)],
        scratch_shapes=[pltpu.VMEM((tm, k), BF16),
                        pltpu.VMEM((tm, k), BF16),
                        pltpu.VMEM((NORM_ROWS, k), BF16),
                        pltpu.VMEM((2, tm + CARRY_ROWS, tn), F32),
                        pltpu.VMEM((2, tm + CARRY_ROWS, tn), F32),
                        pltpu.VMEM((2 * nj, CARRY_ROWS, tn), F32)],
        compiler_params=_params("arbitrary"),
        name="upproj_conv_gate",
    )(x, g.reshape(1, k), h0, w, w, conv_p, conv_p, w_down)


def _down_body(a_ref, w_ref, x_ref, o_ref):
    o_ref[...] = x_ref[...] + jnp.dot(a_ref[...], w_ref[...], preferred_element_type=F32)


def _downproj(a, w, x, tm=512, tn=512):
    m, k = a.shape
    n = w.shape[1]
    return pl.pallas_call(
        _down_body,
        grid=(m // tm, n // tn),
        in_specs=[pl.BlockSpec((tm, k), lambda i, j: (i, 0)),
                  pl.BlockSpec((k, tn), lambda i, j: (0, j)),
                  pl.BlockSpec((tm, tn), lambda i, j: (i, j))],
        out_specs=pl.BlockSpec((tm, tn), lambda i, j: (i, j)),
        out_shape=jax.ShapeDtypeStruct((m, n), F32),
        compiler_params=_params("parallel", "arbitrary"),
        name="downproj",
    )(a, w, x)


def kernel(x, attn_norm_g, w_in, sgu_norm_g, w_spatial, b_spatial, mix_norm_g, w_out, ffn_norm_g,
           w_up, conv_w, conv_b, w_down, final_norm_g):
    b, s, d = x.shape
    depth = w_in.shape[0]
    xm = x.reshape(b * s, d)
    slopes = 2.0 ** (-8.0 * jnp.arange(1, N_HEADS + 1, dtype=F32) / N_HEADS)
    slopes = jnp.broadcast_to(slopes[:, None, None], (N_HEADS, 1, HEAD_DIM))
    conv_p = jnp.concatenate([conv_w, conv_b[:, None, :]], axis=1)
    for i in range(depth):
        mix_g = mix_norm_g[i].reshape(1, D_MODEL)
        h = _rmsnorm(xm, attn_norm_g[i], BF16)
        z = _inproj(h, w_in, i)
        a = _attention(z.reshape(b, s, N_IN), slopes, mix_g).reshape(b * s, D_ATTN)
        g, w_out_bf = _sgu(z, sgu_norm_g[i], w_spatial[i], b_spatial[i].T, mix_g, w_out, i)
        xm = _outproj(a, g, w_out_bf, xm)
        act, w_down_bf = _upproj(xm, ffn_norm_g[i], w_up, conv_p, w_down, i, s)
        xm = _downproj(act, w_down_bf, xm)
    return _rmsnorm(xm, final_norm_g, F32).reshape(b, s, d)
```

```python
import functools
import math

import jax
import jax.numpy as jnp
from jax import lax
from jax.experimental import pallas as pl
from jax.experimental.pallas import tpu as pltpu

F32 = jnp.float32
BF16 = jnp.bfloat16

D_MODEL = 4096
HEAD_DIM = 128
D_ATTN = D_MODEL // 2
D_SGU = D_MODEL - D_ATTN
N_HEADS = D_ATTN // HEAD_DIM
N_GROUPS = D_SGU // HEAD_DIM
CHUNK = 128
BLOCK = 128
DILATIONS = (1, 4, 16)
N_IN = 3 * D_ATTN + 2 * D_SGU
D_FF = 256 * ((8 * D_MODEL // 3 + 255) // 256)
CONV_WIDTH = 3
EPS = 1e-6
LOG2E = math.log2(math.e)
VMEM_LIMIT = 61 * 1024 * 1024
CARRY_ROWS = 8
EPI_ROWS = 32
K_SPLIT = 16
NORM_ROWS = 32
SGU_BATCH = 4
ATT_GROUP = 32


def _params(*sem):
    return pltpu.CompilerParams(dimension_semantics=sem, vmem_limit_bytes=VMEM_LIMIT)


def _rmsnorm_body(x_ref, g_ref, o_ref):
    x = x_ref[...]
    y = x * lax.rsqrt(jnp.mean(x * x, axis=-1, keepdims=True) + EPS)
    o_ref[...] = (y * g_ref[...]).astype(o_ref.dtype)


def _rmsnorm(x, g, out_dtype, tm=512, rows=None):
    m, d = x.shape
    m = m if rows is None else rows
    return pl.pallas_call(
        _rmsnorm_body,
        grid=(m // tm,),
        in_specs=[pl.BlockSpec((tm, d), lambda i: (i, 0)),
                  pl.BlockSpec((1, d), lambda i: (0, 0))],
        out_specs=pl.BlockSpec((tm, d), lambda i: (i, 0)),
        out_shape=jax.ShapeDtypeStruct((m, d), out_dtype),
        compiler_params=_params("parallel"),
        name="rmsnorm",
    )(x, g.reshape(1, d))


def _inproj_body(h_ref, w_ref, o_ref):
    w = w_ref[...].astype(BF16)
    o_ref[...] = jnp.dot(h_ref[...], w, preferred_element_type=F32).astype(o_ref.dtype)


def _inproj(h, w, layer, tm=2048, tn=512):
    m, k = h.shape
    n = w.shape[2]
    return pl.pallas_call(
        _inproj_body,
        grid=(m // tm, n // tn),
        in_specs=[pl.BlockSpec((tm, k), lambda i, j: (i, 0)),
                  pl.BlockSpec((None, k, tn), lambda i, j: (layer, 0, j))],
        out_specs=pl.BlockSpec((tm, tn), lambda i, j: (i, j)),
        out_shape=jax.ShapeDtypeStruct((m, n), BF16),
        compiler_params=_params("parallel", "arbitrary"),
        name="inproj",
    )(h, w)


def _aligned(i, m):
    return i if isinstance(i, int) else pl.multiple_of(i, m)


def _attn_body(q_ref, k_ref, v_ref, slope_ref, g_ref, o_ref, xf, x4f, qd, kd, vd, bias2, ob, mb, lb,
               *, seq):
    qscale = HEAD_DIM ** -0.5 * LOG2E
    d1, d2 = DILATIONS[1], DILATIONS[2]
    assert d2 == d1 * d1 and DILATIONS[0] == 1
    sub1, sub2 = seq // d1, seq // d2
    for src, dst, mul, first_c in ((q_ref, qd, qscale, 0), (k_ref, kd, None, 1), (v_ref, vd, None, 1)):
        if first_c == 0:
            xf[...] = src[...].astype(F32) * mul
            dst[0] = xf[...].astype(BF16)
        else:
            xf[...] = src[...].astype(F32)

        def deint1(r, carry, dst=dst, slot=1 - first_c):
            rows = pl.ds(pl.multiple_of(r * sub1, sub1), sub1)
            x = xf[pl.ds(r, sub1, stride=d1), :]
            x4f[rows, :] = x
            dst[slot, rows, :] = x.astype(BF16)
            return carry

        lax.fori_loop(0, d1, deint1, 0)

        def deint2(r, carry, dst=dst, slot=2 - first_c):
            rows = pl.ds(pl.multiple_of(r * sub2, sub2), sub2)
            x = x4f[pl.ds((r % d1) * sub1 + r // d1, sub2, stride=d1), :]
            dst[slot, rows, :] = x.astype(BF16)
            return carry

        lax.fori_loop(0, d2, deint2, 0)

    slope = slope_ref[...]
    qi = lax.broadcasted_iota(jnp.int32, (BLOCK, 2 * BLOCK), 0)
    kj = lax.broadcasted_iota(jnp.int32, (BLOCK, 2 * BLOCK), 1)
    dist = qi + BLOCK - kj
    valid = (dist >= 0) & (dist <= BLOCK)
    for c, d in enumerate(DILATIONS):
        bias2[c] = jnp.where(valid, (-LOG2E * slope[:, :1]) * (dist * d).astype(F32), -jnp.inf)
    ones = jnp.ones((2 * BLOCK, HEAD_DIM), BF16)

    def group(c, d, blocks):
        getk = (lambda rows: k_ref[rows, :]) if c == 0 else (lambda rows: kd[c - 1, rows, :])
        getv = (lambda rows: v_ref[rows, :]) if c == 0 else (lambda rows: vd[c - 1, rows, :])
        qs, ks, vs = [], [], []
        for gi, _, first in blocks:
            qrows = pl.ds(_aligned(gi * BLOCK, BLOCK), BLOCK)
            krows = qrows if first else pl.ds(_aligned(gi * BLOCK - BLOCK, BLOCK), 2 * BLOCK)
            qs.append(qd[c, qrows, :])
            ks.append(getk(krows))
            v = getv(krows)
            vs.append(jnp.concatenate([v, ones[:v.shape[0]]], axis=1))
        ss = [lax.dot_general(q, k, (((1,), (1,)), ((), ())), preferred_element_type=F32)
              for q, k in zip(qs, ks)]
        ps, ms = [], []
        for s, (_, _, first) in zip(ss, blocks):
            s = s + (bias2[c, :, BLOCK:] if first else bias2[c])
            mx = jnp.max(s, axis=-1, keepdims=True)
            ms.append(mx)
            ps.append(jnp.exp2(s - mx).astype(BF16))
        os = [jnp.dot(p, v, preferred_element_type=F32) for p, v in zip(ps, vs)]
        for o, mx, (_, rows, _) in zip(os, ms, blocks):
            ob[c, rows, :] = o[:, :HEAD_DIM]
            lb[c, rows, :] = o[:, HEAD_DIM:]
            mb[c, rows, :] = jnp.broadcast_to(mx, (BLOCK, HEAD_DIM))

    for c, d in enumerate(DILATIONS):
        nb = seq // d // BLOCK

        def blk(r, n, first, d=d, nb=nb):
            gi = r * nb + n
            if d <= d1:
                return (gi, pl.ds(_aligned(gi * BLOCK, BLOCK), BLOCK), first)
            q = d // d1
            start = (r % d1) * sub1 + q * BLOCK * n + r // d1
            return (gi, pl.ds(start, BLOCK, stride=q), first)

        if nb >= ATT_GROUP:
            def r_body(r, carry, c=c, d=d, nb=nb, blk=blk):
                group(c, d, [blk(r, n, n == 0) for n in range(ATT_GROUP)])

                def n_body(i, carry2):
                    group(c, d, [blk(r, i * ATT_GROUP + t, False) for t in range(ATT_GROUP)])
                    return carry2

                return lax.fori_loop(1, nb // ATT_GROUP, n_body, carry)

            if d == 1:
                r_body(0, 0)
            else:
                lax.fori_loop(0, d, r_body, 0)
        else:
            per = ATT_GROUP // nb

            def r_body(i, carry, c=c, d=d, nb=nb, blk=blk, per=per):
                group(c, d, [blk(i * per + t, n, n == 0) for t in range(per) for n in range(nb)])
                return carry

            lax.fori_loop(0, d // per, r_body, 0)

    g = g_ref[...]
    step, piece = 512, 64

    def fin(t, carry):
        r = (t * step) // sub1
        m0 = (t * step) % sub1
        for p0 in range(0, step, piece):
            rows = pl.ds(pl.multiple_of(t * step + p0, piece), piece)
            toks = pl.ds(r + d1 * (m0 + p0), piece, stride=d1)
            ma, mb1, mb2 = mb[0, toks, :], mb[1, rows, :], mb[2, rows, :]
            mx = jnp.maximum(jnp.maximum(ma, mb1), mb2)
            e0, e1, e2 = jnp.exp2(ma - mx), jnp.exp2(mb1 - mx), jnp.exp2(mb2 - mx)
            den = e0 * lb[0, toks, :] + e1 * lb[1, rows, :] + e2 * lb[2, rows, :]
            num = e0 * ob[0, toks, :] + e1 * ob[1, rows, :] + e2 * ob[2, rows, :]
            msq = jnp.mean(num * num, axis=-1, keepdims=True)
            xf[toks, :] = num * lax.rsqrt(msq + EPS * (den * den)) * g
        return carry

    lax.fori_loop(0, seq // step, fin, 0)
    o_ref[...] = xf[...].astype(o_ref.dtype)


def _attention(z3, slopes, mix_g):
    b, s, _ = z3.shape
    blk = lambda off: pl.BlockSpec((None, s, HEAD_DIM), lambda bi, h, off=off: (bi, 0, off + h))
    return pl.pallas_call(
        functools.partial(_attn_body, seq=s),
        grid=(b, N_HEADS),
        in_specs=[blk(0), blk(N_HEADS), blk(2 * N_HEADS),
                  pl.BlockSpec((None, 1, HEAD_DIM), lambda bi, h: (h, 0, 0)),
                  pl.BlockSpec((1, HEAD_DIM), lambda bi, h: (0, h))],
        out_specs=pl.BlockSpec((None, s, HEAD_DIM), lambda bi, h: (bi, 0, h)),
        out_shape=jax.ShapeDtypeStruct((b, s, D_ATTN), BF16),
        scratch_shapes=[pltpu.VMEM((s, HEAD_DIM), F32)] * 2
        + [pltpu.VMEM((len(DILATIONS), s, HEAD_DIM), BF16)]
        + [pltpu.VMEM((len(DILATIONS) - 1, s, HEAD_DIM), BF16)] * 2
        + [pltpu.VMEM((len(DILATIONS), BLOCK, 2 * BLOCK), F32)]
        + [pltpu.VMEM((len(DILATIONS), s, HEAD_DIM), F32)] * 3,
        compiler_params=_params("parallel", "parallel"),
        name="dilated_attn",
    )(z3, z3, z3, slopes, mix_g)


def _gelu(x):
    return 0.5 * x * (1.0 + lax.erf(x * math.sqrt(0.5)))


def _sgu_body(u_ref, v_ref, ng_ref, w_ref, bt_ref, mg_ref, wo_ref, o_ref, wob_ref, *, tm):
    wob_ref[...] = wo_ref[...].astype(wob_ref.dtype)
    ti = lax.broadcasted_iota(jnp.int32, (CHUNK, CHUNK), 0)
    si = lax.broadcasted_iota(jnp.int32, (CHUNK, CHUNK), 1)
    causal = si <= ti
    n_ch = tm // CHUNK
    for g0 in range(0, N_GROUPS, SGU_BATCH):
        groups = range(g0, g0 + SGU_BATCH)
        vcat = []
        for g in groups:
            cols = slice(g * HEAD_DIM, (g + 1) * HEAD_DIM)
            ng = ng_ref[:, cols]
            vs = []
            for ch in range(n_ch):
                v = _gelu(v_ref[ch * CHUNK:(ch + 1) * CHUNK, cols].astype(F32))
                v = v * lax.rsqrt(jnp.mean(v * v, axis=-1, keepdims=True) + EPS) * ng
                vs.append(v.astype(BF16))
            vcat.append(jnp.concatenate(vs, axis=1))
        mixed = [jnp.dot(jnp.where(causal, w_ref[g], 0.0).astype(BF16), vc, preferred_element_type=F32)
                 for g, vc in zip(groups, vcat)]
        for g, mx in zip(groups, mixed):
            cols = slice(g * HEAD_DIM, (g + 1) * HEAD_DIM)
            bcol = bt_ref[:, g:g + 1]
            mg = mg_ref[:, cols]
            for ch in range(n_ch):
                rows = slice(ch * CHUNK, (ch + 1) * CHUNK)
                u = _gelu(u_ref[rows, cols].astype(F32))
                y = u * (mx[:, ch * HEAD_DIM:(ch + 1) * HEAD_DIM] + bcol)
                y = y * lax.rsqrt(jnp.mean(y * y, axis=-1, keepdims=True) + EPS) * mg
                o_ref[rows, cols] = y.astype(o_ref.dtype)


def _sgu(z, sgu_norm_g, w_spatial, b_spatial_t, mix_g, w_out, layer, tm=256):
    m = z.shape[0]
    steps = m // tm
    k_out, n_out = w_out.shape[1:]
    slab = k_out // steps
    assert slab * steps == k_out and slab % 16 == 0
    ublk = 3 * D_ATTN // D_SGU
    return pl.pallas_call(
        functools.partial(_sgu_body, tm=tm),
        grid=(steps,),
        in_specs=[pl.BlockSpec((tm, D_SGU), lambda i: (i, ublk)),
                  pl.BlockSpec((tm, D_SGU), lambda i: (i, ublk + 1)),
                  pl.BlockSpec((1, D_SGU), lambda i: (0, 0)),
                  pl.BlockSpec((N_GROUPS, CHUNK, CHUNK), lambda i: (0, 0, 0)),
                  pl.BlockSpec((CHUNK, N_GROUPS), lambda i: (0, 0)),
                  pl.BlockSpec((1, D_SGU), lambda i: (0, 1)),
                  pl.BlockSpec((None, slab, n_out), lambda i: (layer, i, 0))],
        out_specs=[pl.BlockSpec((tm, D_SGU), lambda i: (i, 0)),
                   pl.BlockSpec((slab, n_out), lambda i: (i, 0))],
        out_shape=[jax.ShapeDtypeStruct((m, D_SGU), BF16),
                   jax.ShapeDtypeStruct((k_out, n_out), BF16)],
        compiler_params=_params("parallel"),
        name="sgu",
    )(z, z, sgu_norm_g.reshape(1, D_SGU), w_spatial, b_spatial_t, mix_g, w_out)


def _outproj_body(a_ref, g_ref, wa_ref, wg_ref, x_ref, o_ref):
    acc = jnp.dot(a_ref[...], wa_ref[...], preferred_element_type=F32)
    acc = acc + jnp.dot(g_ref[...], wg_ref[...], preferred_element_type=F32)
    o_ref[...] = x_ref[...] + acc


def _outproj(a, g, w, x, tm=1024, tn=1024):
    m, ka = a.shape
    kg = g.shape[1]
    assert ka == kg
    n = w.shape[1]
    return pl.pallas_call(
        _outproj_body,
        grid=(m // tm, n // tn),
        in_specs=[pl.BlockSpec((tm, ka), lambda i, j: (i, 0)),
                  pl.BlockSpec((tm, kg), lambda i, j: (i, 0)),
                  pl.BlockSpec((ka, tn), lambda i, j: (0, j)),
                  pl.BlockSpec((kg, tn), lambda i, j: (1, j)),
                  pl.BlockSpec((tm, tn), lambda i, j: (i, j))],
        out_specs=pl.BlockSpec((tm, tn), lambda i, j: (i, j)),
        out_shape=jax.ShapeDtypeStruct((m, n), F32),
        compiler_params=_params("parallel", "arbitrary"),
        name="outproj",
    )(a, g, w, w, x)


def _up_body(x_ref, g_ref, h0_ref, wg_ref, wv_ref, cpg_ref, cpv_ref, wd_ref,
             o_ref, wdo_ref, h_a, h_b, h_stage, ext_a, ext_b, carry_ref, *, tm, nj, n_tiles, blocks_per_seq):
    t = pl.program_id(0)
    te = jnp.maximum(t - 1, 0)
    je = te % nj
    seq_start = ((te // nj) % blocks_per_seq) == 0
    tmm = jnp.minimum(t, n_tiles - 1)
    cur = (tmm // nj) % 2
    jn = jnp.minimum(tmm % nj, tm // NORM_ROWS - 1)

    @pl.when(t == 0)
    def _():
        ext_a[...] = jnp.zeros_like(ext_a)
        ext_b[...] = jnp.zeros_like(ext_b)
        carry_ref[...] = jnp.zeros_like(carry_ref)
        h_a[...] = h0_ref[...]

    def stage(ext_w, ext_r, h_cur):
        for half in range(2):
            slot = half * nj + je
            prev = jnp.where(seq_start, 0.0, carry_ref[slot])
            carry_ref[slot] = ext_r[half, tm:tm + CARRY_ROWS, :]
            ext_r[half, 0:CARRY_ROWS, :] = prev

        def conv(half, r0, cp_ref):
            slab = ext_r[half, r0:r0 + CARRY_ROWS + EPI_ROWS, :]
            y = cp_ref[CONV_WIDTH:CONV_WIDTH + 1, :]
            for j in range(CONV_WIDTH - 1, 0, -1):
                tap = pltpu.roll(slab, j, axis=0)[CARRY_ROWS:, :]
                y = y + cp_ref[CONV_WIDTH - 1 - j:CONV_WIDTH - j, :] * tap
            return y + cp_ref[CONV_WIDTH - 1:CONV_WIDTH, :] * slab[CARRY_ROWS:, :]

        kc = h_cur.shape[1] // K_SPLIT
        rows_per_piece = tm // (2 * K_SPLIT)
        piece = 0
        for half, w_ref in enumerate((wg_ref, wv_ref)):
            acc = None
            for p in range(K_SPLIT):
                ks = slice(p * kc, (p + 1) * kc)
                part = jnp.dot(h_cur[:, ks], w_ref[ks, :].astype(BF16), preferred_element_type=F32)
                acc = part if acc is None else acc + part
                for r0 in range(piece * rows_per_piece, (piece + 1) * rows_per_piece, EPI_ROWS):
                    gate = conv(0, r0, cpg_ref)
                    val = conv(1, r0, cpv_ref)
                    o_ref[r0:r0 + EPI_ROWS, :] = (jax.nn.silu(gate) * val).astype(o_ref.dtype)
                piece += 1
                if piece == K_SPLIT // 2:
                    wdo_ref[...] = wd_ref[...].astype(wdo_ref.dtype)
            ext_w[half, CARRY_ROWS:, :] = acc
            if half == 0:
                xs = x_ref[...]
                width = xs.shape[1]
                part = None
                for c0 in range(0, width, HEAD_DIM):
                    sq = xs[:, c0:c0 + HEAD_DIM] * xs[:, c0:c0 + HEAD_DIM]
                    part = sq if part is None else part + sq
                hi = part.astype(BF16)
                lo = (part - hi.astype(F32)).astype(BF16)
                ones = jnp.ones((HEAD_DIM, HEAD_DIM), BF16)
                ssq = (jnp.dot(hi, ones, preferred_element_type=F32)
                       + jnp.dot(lo, ones, preferred_element_type=F32))
                rs = lax.rsqrt(ssq * (1.0 / width) + EPS)
                for c0 in range(0, width, HEAD_DIM):
                    cols = slice(c0, c0 + HEAD_DIM)
                    h_stage[:, cols] = (xs[:, cols] * rs * g_ref[:, cols]).astype(h_stage.dtype)

    for tp, (ext_w, ext_r) in enumerate(((ext_a, ext_b), (ext_b, ext_a))):
        for hp, h_cur in enumerate((h_a, h_b)):
            pl.when((t % 2 == tp) & (cur == hp))(functools.partial(stage, ext_w, ext_r, h_cur))

    rows = pl.ds(pl.multiple_of(jn * NORM_ROWS, NORM_ROWS), NORM_ROWS)
    fresh = tmm % nj < tm // NORM_ROWS
    for hp, h_next in enumerate((h_b, h_a)):
        @pl.when((cur == hp) & fresh)
        def _(h_next=h_next):
            h_next[rows, :] = h_stage[...]


def _upproj(x, g, w, conv_p, w_down, layer, seq, tm=1024, tn=256):
    m, k = x.shape
    assert seq % tm == 0 and m % seq == 0 and D_FF % tn == 0 and k % (K_SPLIT * 256) == 0
    nj = D_FF // tn
    n_tiles = (m // tm) * nj
    n_down = w_down.shape[2]
    slab = D_FF // n_tiles
    assert slab * n_tiles == D_FF and slab % 16 == 0
    slabs_per_block = tm // NORM_ROWS
    assert slabs_per_block <= nj
    n_slabs = m // NORM_ROWS
    h0 = _rmsnorm(x, g, BF16, rows=tm)
    body = functools.partial(_up_body, tm=tm, nj=nj, n_tiles=n_tiles, blocks_per_seq=seq // tm)
    mm = lambda t: jnp.minimum(t, n_tiles - 1)
    ep = lambda t: jnp.maximum(t - 1, 0)

    def next_slab(t):
        s = (mm(t) // nj + 1) * slabs_per_block + jnp.minimum(mm(t) % nj, slabs_per_block - 1)
        return (jnp.minimum(s, n_slabs - 1), 0)

    return pl.pallas_call(
        body,
        grid=(n_tiles + 1,),
        in_specs=[pl.BlockSpec((NORM_ROWS, k), next_slab),
                  pl.BlockSpec((1, k), lambda t: (0, 0)),
                  pl.BlockSpec((tm, k), lambda t: (0, 0), pipeline_mode=pl.Buffered(1)),
                  pl.BlockSpec((None, k, tn), lambda t: (layer, 0, mm(t) % nj)),
                  pl.BlockSpec((None, k, tn), lambda t: (layer, 0, mm(t) % nj + nj)),
                  pl.BlockSpec((None, CONV_WIDTH + 1, tn), lambda t: (layer, 0, ep(t) % nj)),
                  pl.BlockSpec((None, CONV_WIDTH + 1, tn), lambda t: (layer, 0, ep(t) % nj + nj)),
                  pl.BlockSpec((None, slab, n_down), lambda t: (layer, mm(t), 0))],
        out_specs=[pl.BlockSpec((tm, tn), lambda t: (ep(t) // nj, ep(t) % nj)),
                   pl.BlockSpec((slab, n_down), lambda t: (mm(t), 0))],
        out_shape=[jax.ShapeDtypeStruct((m, D_FF), BF16),
                   jax.ShapeDtypeStruct((D_FF, n_down), BF16)],
        scratch_shapes=[pltpu.VMEM((tm, k), BF16),
                        pltpu.VMEM((tm, k), BF16),
                        pltpu.VMEM((NORM_ROWS, k), BF16),
                        pltpu.VMEM((2, tm + CARRY_ROWS, tn), F32),
                        pltpu.VMEM((2, tm + CARRY_ROWS, tn), F32),
                        pltpu.VMEM((2 * nj, CARRY_ROWS, tn), F32)],
        compiler_params=_params("arbitrary"),
        name="upproj_conv_gate",
    )(x, g.reshape(1, k), h0, w, w, conv_p, conv_p, w_down)


def _down_body(a_ref, w_ref, x_ref, o_ref):
    o_ref[...] = x_ref[...] + jnp.dot(a_ref[...], w_ref[...], preferred_element_type=F32)


def _downproj(a, w, x, tm=1024, tn=256):
    m, k = a.shape
    n = w.shape[1]
    return pl.pallas_call(
        _down_body,
        grid=(m // tm, n // tn),
        in_specs=[pl.BlockSpec((tm, k), lambda i, j: (i, 0)),
                  pl.BlockSpec((k, tn), lambda i, j: (0, j)),
                  pl.BlockSpec((tm, tn), lambda i, j: (i, j))],
        out_specs=pl.BlockSpec((tm, tn), lambda i, j: (i, j)),
        out_shape=jax.ShapeDtypeStruct((m, n), F32),
        compiler_params=_params("parallel", "arbitrary"),
        name="downproj",
    )(a, w, x)


def kernel(x, attn_norm_g, w_in, sgu_norm_g, w_spatial, b_spatial, mix_norm_g, w_out, ffn_norm_g,
           w_up, conv_w, conv_b, w_down, final_norm_g):
    b, s, d = x.shape
    depth = w_in.shape[0]
    xm = x.reshape(b * s, d)
    slopes = 2.0 ** (-8.0 * jnp.arange(1, N_HEADS + 1, dtype=F32) / N_HEADS)
    slopes = jnp.broadcast_to(slopes[:, None, None], (N_HEADS, 1, HEAD_DIM))
    conv_p = jnp.concatenate([conv_w, conv_b[:, None, :]], axis=1)
    for i in range(depth):
        mix_g = mix_norm_g[i].reshape(1, D_MODEL)
        h = _rmsnorm(xm, attn_norm_g[i], BF16)
        z = _inproj(h, w_in, i)
        a = _attention(z.reshape(b, s, N_IN), slopes, mix_g).reshape(b * s, D_ATTN)
        g, w_out_bf = _sgu(z, sgu_norm_g[i], w_spatial[i], b_spatial[i].T, mix_g, w_out, i)
        xm = _outproj(a, g, w_out_bf, xm)
        act, w_down_bf = _upproj(xm, ffn_norm_g[i], w_up, conv_p, w_down, i, s)
        xm = _downproj(act, w_down_bf, xm)
    return _rmsnorm(xm, final_norm_g, F32).reshape(b, s, d)
```

```python
import functools
import math

import jax
import jax.numpy as jnp
from jax import lax
from jax.experimental import pallas as pl
from jax.experimental.pallas import tpu as pltpu

F32 = jnp.float32
BF16 = jnp.bfloat16

D_MODEL = 4096
HEAD_DIM = 128
D_ATTN = D_MODEL // 2
D_SGU = D_MODEL - D_ATTN
N_HEADS = D_ATTN // HEAD_DIM
N_GROUPS = D_SGU // HEAD_DIM
CHUNK = 128
BLOCK = 128
DILATIONS = (1, 4, 16)
N_IN = 3 * D_ATTN + 2 * D_SGU
D_FF = 256 * ((8 * D_MODEL // 3 + 255) // 256)
CONV_WIDTH = 3
EPS = 1e-6
LOG2E = math.log2(math.e)
QSCALE = HEAD_DIM ** -0.5 * LOG2E
VMEM_LIMIT = 61 * 1024 * 1024
CARRY_ROWS = 8
EPI_ROWS = 32
K_SPLIT = 16
NORM_ROWS = 32
SGU_BATCH = 4

def _params(*sem):
    return pltpu.CompilerParams(dimension_semantics=sem, vmem_limit_bytes=VMEM_LIMIT)


def _rmsnorm_body(x_ref, g_ref, o_ref):
    x = x_ref[...]
    y = x * lax.rsqrt(jnp.mean(x * x, axis=-1, keepdims=True) + EPS)
    o_ref[...] = (y * g_ref[...]).astype(o_ref.dtype)


def _rmsnorm(x, g, out_dtype, tm=512, rows=None):
    m, d = x.shape
    m = m if rows is None else rows
    return pl.pallas_call(
        _rmsnorm_body,
        grid=(m // tm,),
        in_specs=[pl.BlockSpec((tm, d), lambda i: (i, 0)),
                  pl.BlockSpec((1, d), lambda i: (0, 0))],
        out_specs=pl.BlockSpec((tm, d), lambda i: (i, 0)),
        out_shape=jax.ShapeDtypeStruct((m, d), out_dtype),
        compiler_params=_params("parallel"),
        name="rmsnorm",
    )(x, g.reshape(1, d))


def _inproj_body(h_ref, w_ref, o_ref, *, q_tiles):
    w = w_ref[...].astype(BF16)
    scale = jnp.where(pl.program_id(1) < q_tiles, QSCALE, 1.0)
    o_ref[...] = (jnp.dot(h_ref[...], w, preferred_element_type=F32) * scale).astype(o_ref.dtype)


def _inproj(h, w, layer, tm=2048, tn=512):
    m, k = h.shape
    n = w.shape[2]
    assert D_ATTN % tn == 0
    return pl.pallas_call(
        functools.partial(_inproj_body, q_tiles=D_ATTN // tn),
        grid=(m // tm, n // tn),
        in_specs=[pl.BlockSpec((tm, k), lambda i, j: (i, 0)),
                  pl.BlockSpec((None, k, tn), lambda i, j: (layer, 0, j))],
        out_specs=pl.BlockSpec((tm, tn), lambda i, j: (i, j)),
        out_shape=jax.ShapeDtypeStruct((m, n), BF16),
        compiler_params=_params("parallel", "arbitrary"),
        name="inproj",
    )(h, w)


def _attn_body(*refs, seq):
    d1, d2 = DILATIONS[1], DILATIONS[2]
    assert d2 == d1 * d1 and DILATIONS[0] == 1
    q_ref, k_ref, v_ref = refs[:3]
    q4, k4, v4 = refs[3:3 + d1], refs[3 + d1:3 + 2 * d1], refs[3 + 2 * d1:3 + 3 * d1]
    slope_ref, g_ref, o_ref, xf, x4f, q16, k16, v16, bias2, ob, mb, lb = refs[3 + 3 * d1:]
    sub1, sub2 = seq // d1, seq // d2
    for src4, dst in ((q4, q16), (k4, k16), (v4, v16)):
        for r in range(d1):
            x4f[r * sub1:(r + 1) * sub1, :] = src4[r][...].astype(F32)

        def deint2(r, carry, dst=dst):
            rows = pl.ds(pl.multiple_of(r * sub2, sub2), sub2)
            x = x4f[pl.ds((r % d1) * sub1 + r // d1, sub2, stride=d1), :]
            dst[rows, :] = x.astype(BF16)
            return carry

        lax.fori_loop(0, d2, deint2, 0)

    slope = slope_ref[...]
    qi = lax.broadcasted_iota(jnp.int32, (BLOCK, 2 * BLOCK), 0)
    kj = lax.broadcasted_iota(jnp.int32, (BLOCK, 2 * BLOCK), 1)
    dist = qi + BLOCK - kj
    valid = (dist >= 0) & (dist <= BLOCK)
    for c, d in enumerate(DILATIONS):
        bias2[c] = jnp.where(valid, (-LOG2E * slope[:, :1]) * (dist * d).astype(F32), -jnp.inf)
    ones = jnp.ones((2 * BLOCK, HEAD_DIM), BF16)

    def group(c, blocks):
        nb = seq // DILATIONS[c] // BLOCK
        qs, ks, vs = [], [], []
        for r, n, _ in blocks:
            first = n == 0
            if c == 1:
                qsrc, ksrc, vsrc, base = q4[r], k4[r], v4[r], n * BLOCK
            else:
                qsrc, ksrc, vsrc = ((q_ref, k_ref, v_ref), None, (q16, k16, v16))[c]
                base = (r * nb + n) * BLOCK
            qrows = slice(base, base + BLOCK)
            krows = qrows if first else slice(base - BLOCK, base + BLOCK)
            qs.append(qsrc[qrows, :])
            ks.append(ksrc[krows, :])
            v = vsrc[krows, :]
            vs.append(jnp.concatenate([v, ones[:v.shape[0]]], axis=1))
        ss = [lax.dot_general(q, k, (((1,), (1,)), ((), ())), preferred_element_type=F32)
              for q, k in zip(qs, ks)]
        ps, ms = [], []
        for s, (_, n, _) in zip(ss, blocks):
            s = s + (bias2[c, :, BLOCK:] if n == 0 else bias2[c])
            mx = jnp.max(s, axis=-1, keepdims=True)
            ms.append(mx)
            ps.append(jnp.exp2(s - mx).astype(BF16))
        os = [jnp.dot(p, v, preferred_element_type=F32) for p, v in zip(ps, vs)]
        for o, mx, (_, _, rows) in zip(os, ms, blocks):
            ob[c, rows, :] = o[:, :HEAD_DIM]
            lb[c, rows, :] = o[:, HEAD_DIM:]
            mb[c, rows, :] = jnp.broadcast_to(mx, (BLOCK, HEAD_DIM))

    for c, d in enumerate(DILATIONS):
        nb = seq // d // BLOCK

        def blk(r, n, d=d, nb=nb):
            if d <= d1:
                start = (r * nb + n) * BLOCK
                return (r, n, pl.ds(start, BLOCK))
            q = d // d1
            return (r, n, pl.ds((r % d1) * sub1 + q * BLOCK * n + r // d1, BLOCK, stride=q))

        group(c, [blk(r, n) for r in range(d) for n in range(nb)])

    g = g_ref[...]
    step, piece = 512, 64

    def fin(t, carry):
        r = (t * step) // sub1
        m0 = (t * step) % sub1
        for p0 in range(0, step, piece):
            rows = pl.ds(pl.multiple_of(t * step + p0, piece), piece)
            toks = pl.ds(r + d1 * (m0 + p0), piece, stride=d1)
            ma, mb1, mb2 = mb[0, toks, :], mb[1, rows, :], mb[2, rows, :]
            mx = jnp.maximum(jnp.maximum(ma, mb1), mb2)
            e0, e1, e2 = jnp.exp2(ma - mx), jnp.exp2(mb1 - mx), jnp.exp2(mb2 - mx)
            den = e0 * lb[0, toks, :] + e1 * lb[1, rows, :] + e2 * lb[2, rows, :]
            num = e0 * ob[0, toks, :] + e1 * ob[1, rows, :] + e2 * ob[2, rows, :]
            msq = jnp.mean(num * num, axis=-1, keepdims=True)
            xf[toks, :] = num * lax.rsqrt(msq + EPS * (den * den)) * g
        return carry

    lax.fori_loop(0, seq // step, fin, 0)
    o_ref[...] = xf[...].astype(o_ref.dtype)


def _attention(z3, slopes, mix_g):
    b, s, n_in = z3.shape
    d1 = DILATIONS[1]
    per_row = n_in // HEAD_DIM
    z4 = z3.reshape(b, s // d1, d1 * n_in)
    nat = lambda off: pl.BlockSpec((None, s, HEAD_DIM), lambda bi, h: (bi, 0, off + h))
    sub = lambda off, r: pl.BlockSpec((None, s // d1, HEAD_DIM), lambda bi, h: (bi, 0, r * per_row + off + h))
    offs = (0, N_HEADS, 2 * N_HEADS)
    return pl.pallas_call(
        functools.partial(_attn_body, seq=s),
        grid=(b, N_HEADS),
        in_specs=[nat(off) for off in offs]
        + [sub(off, r) for off in offs for r in range(d1)]
        + [pl.BlockSpec((None, 1, HEAD_DIM), lambda bi, h: (h, 0, 0)),
           pl.BlockSpec((1, HEAD_DIM), lambda bi, h: (0, h))],
        out_specs=pl.BlockSpec((None, s, HEAD_DIM), lambda bi, h: (bi, 0, h)),
        out_shape=jax.ShapeDtypeStruct((b, s, D_ATTN), BF16),
        scratch_shapes=[pltpu.VMEM((s, HEAD_DIM), F32)] * 2
        + [pltpu.VMEM((s, HEAD_DIM), BF16)] * 3
        + [pltpu.VMEM((len(DILATIONS), BLOCK, 2 * BLOCK), F32)]
        + [pltpu.VMEM((len(DILATIONS), s, HEAD_DIM), F32)] * 3,
        compiler_params=_params("parallel", "parallel"),
        name="dilated_attn",
    )(*([z3] * 3 + [z4] * (3 * d1)), slopes, mix_g)


def _gelu(x):
    return 0.5 * x * (1.0 + lax.erf(x * math.sqrt(0.5)))


def _sgu_body(u_ref, v_ref, ng_ref, w_ref, bt_ref, mg_ref, wo_ref, o_ref, wob_ref, *, tm):
    wob_ref[...] = wo_ref[...].astype(wob_ref.dtype)
    ti = lax.broadcasted_iota(jnp.int32, (CHUNK, CHUNK), 0)
    si = lax.broadcasted_iota(jnp.int32, (CHUNK, CHUNK), 1)
    causal = si <= ti
    n_ch = tm // CHUNK
    for g0 in range(0, N_GROUPS, SGU_BATCH):
        groups = range(g0, g0 + SGU_BATCH)
        vcat = []
        for g in groups:
            cols = slice(g * HEAD_DIM, (g + 1) * HEAD_DIM)
            ng = ng_ref[:, cols]
            vs = []
            for ch in range(n_ch):
                v = _gelu(v_ref[ch * CHUNK:(ch + 1) * CHUNK, cols].astype(F32))
                v = v * lax.rsqrt(jnp.mean(v * v, axis=-1, keepdims=True) + EPS) * ng
                vs.append(v.astype(BF16))
            vcat.append(jnp.concatenate(vs, axis=1))
        mixed = [jnp.dot(jnp.where(causal, w_ref[g], 0.0).astype(BF16), vc, preferred_element_type=F32)
                 for g, vc in zip(groups, vcat)]
        for g, mx in zip(groups, mixed):
            cols = slice(g * HEAD_DIM, (g + 1) * HEAD_DIM)
            bcol = bt_ref[:, g:g + 1]
            mg = mg_ref[:, cols]
            for ch in range(n_ch):
                rows = slice(ch * CHUNK, (ch + 1) * CHUNK)
                u = _gelu(u_ref[rows, cols].astype(F32))
                y = u * (mx[:, ch * HEAD_DIM:(ch + 1) * HEAD_DIM] + bcol)
                y = y * lax.rsqrt(jnp.mean(y * y, axis=-1, keepdims=True) + EPS) * mg
                o_ref[rows, cols] = y.astype(o_ref.dtype)


def _sgu(z, sgu_norm_g, w_spatial, b_spatial_t, mix_g, w_out, layer, tm=256):
    m = z.shape[0]
    steps = m // tm
    k_out, n_out = w_out.shape[1:]
    slab = k_out // steps
    assert slab * steps == k_out and slab % 16 == 0
    ublk = 3 * D_ATTN // D_SGU
    return pl.pallas_call(
        functools.partial(_sgu_body, tm=tm),
        grid=(steps,),
        in_specs=[pl.BlockSpec((tm, D_SGU), lambda i: (i, ublk)),
                  pl.BlockSpec((tm, D_SGU), lambda i: (i, ublk + 1)),
                  pl.BlockSpec((1, D_SGU), lambda i: (0, 0)),
                  pl.BlockSpec((N_GROUPS, CHUNK, CHUNK), lambda i: (0, 0, 0)),
                  pl.BlockSpec((CHUNK, N_GROUPS), lambda i: (0, 0)),
                  pl.BlockSpec((1, D_SGU), lambda i: (0, 1)),
                  pl.BlockSpec((None, slab, n_out), lambda i: (layer, i, 0))],
        out_specs=[pl.BlockSpec((tm, D_SGU), lambda i: (i, 0)),
                   pl.BlockSpec((slab, n_out), lambda i: (i, 0))],
        out_shape=[jax.ShapeDtypeStruct((m, D_SGU), BF16),
                   jax.ShapeDtypeStruct((k_out, n_out), BF16)],
        compiler_params=_params("parallel"),
        name="sgu",
    )(z, z, sgu_norm_g.reshape(1, D_SGU), w_spatial, b_spatial_t, mix_g, w_out)


def _outproj_body(a_ref, g_ref, wa_ref, wg_ref, x_ref, o_ref):
    acc = jnp.dot(a_ref[...], wa_ref[...], preferred_element_type=F32)
    acc = acc + jnp.dot(g_ref[...], wg_ref[...], preferred_element_type=F32)
    o_ref[...] = x_ref[...] + acc


def _outproj(a, g, w, x, tm=1024, tn=1024):
    m, ka = a.shape
    kg = g.shape[1]
    assert ka == kg
    n = w.shape[1]
    return pl.pallas_call(
        _outproj_body,
        grid=(m // tm, n // tn),
        in_specs=[pl.BlockSpec((tm, ka), lambda i, j: (i, 0)),
                  pl.BlockSpec((tm, kg), lambda i, j: (i, 0)),
                  pl.BlockSpec((ka, tn), lambda i, j: (0, j)),
                  pl.BlockSpec((kg, tn), lambda i, j: (1, j)),
                  pl.BlockSpec((tm, tn), lambda i, j: (i, j))],
        out_specs=pl.BlockSpec((tm, tn), lambda i, j: (i, j)),
        out_shape=jax.ShapeDtypeStruct((m, n), F32),
        compiler_params=_params("parallel", "arbitrary"),
        name="outproj",
    )(a, g, w, w, x)


def _up_body(x_ref, g_ref, h0_ref, wg_ref, wv_ref, cpg_ref, cpv_ref, wd_ref,
             o_ref, wdo_ref, h_a, h_b, h_stage, ext_a, ext_b, carry_ref, *, tm, nj, n_tiles, blocks_per_seq):
    t = pl.program_id(0)
    te = jnp.maximum(t - 1, 0)
    je = te % nj
    seq_start = ((te // nj) % blocks_per_seq) == 0
    tmm = jnp.minimum(t, n_tiles - 1)
    cur = (tmm // nj) % 2
    jn = jnp.minimum(tmm % nj, tm // NORM_ROWS - 1)

    @pl.when(t == 0)
    def _():
        ext_a[...] = jnp.zeros_like(ext_a)
        ext_b[...] = jnp.zeros_like(ext_b)
        carry_ref[...] = jnp.zeros_like(carry_ref)
        h_a[...] = h0_ref[...]

    def stage(ext_w, ext_r, h_cur):
        for half in range(2):
            slot = half * nj + je
            prev = jnp.where(seq_start, 0.0, carry_ref[slot])
            carry_ref[slot] = ext_r[half, tm:tm + CARRY_ROWS, :]
            ext_r[half, 0:CARRY_ROWS, :] = prev

        def conv(half, r0, cp_ref):
            slab = ext_r[half, r0:r0 + CARRY_ROWS + EPI_ROWS, :]
            y = cp_ref[CONV_WIDTH:CONV_WIDTH + 1, :]
            for j in range(CONV_WIDTH - 1, 0, -1):
                tap = pltpu.roll(slab, j, axis=0)[CARRY_ROWS:, :]
                y = y + cp_ref[CONV_WIDTH - 1 - j:CONV_WIDTH - j, :] * tap
            return y + cp_ref[CONV_WIDTH - 1:CONV_WIDTH, :] * slab[CARRY_ROWS:, :]

        kc = h_cur.shape[1] // K_SPLIT
        rows_per_piece = tm // (2 * K_SPLIT)
        piece = 0
        for half, w_ref in enumerate((wg_ref, wv_ref)):
            acc = None
            for p in range(K_SPLIT):
                ks = slice(p * kc, (p + 1) * kc)
                part = jnp.dot(h_cur[:, ks], w_ref[ks, :].astype(BF16), preferred_element_type=F32)
                acc = part if acc is None else acc + part
                for r0 in range(piece * rows_per_piece, (piece + 1) * rows_per_piece, EPI_ROWS):
                    gate = conv(0, r0, cpg_ref)
                    val = conv(1, r0, cpv_ref)
                    o_ref[r0:r0 + EPI_ROWS, :] = (jax.nn.silu(gate) * val).astype(o_ref.dtype)
                piece += 1
                if piece == K_SPLIT // 2:
                    wdo_ref[...] = wd_ref[...].astype(wdo_ref.dtype)
            ext_w[half, CARRY_ROWS:, :] = acc
            if half == 0:
                xs = x_ref[...]
                width = xs.shape[1]
                part = None
                for c0 in range(0, width, HEAD_DIM):
                    sq = xs[:, c0:c0 + HEAD_DIM] * xs[:, c0:c0 + HEAD_DIM]
                    part = sq if part is None else part + sq
                hi = part.astype(BF16)
                lo = (part - hi.astype(F32)).astype(BF16)
                ones = jnp.ones((HEAD_DIM, HEAD_DIM), BF16)
                ssq = (jnp.dot(hi, ones, preferred_element_type=F32)
                       + jnp.dot(lo, ones, preferred_element_type=F32))
                rs = lax.rsqrt(ssq * (1.0 / width) + EPS)
                for c0 in range(0, width, HEAD_DIM):
                    cols = slice(c0, c0 + HEAD_DIM)
                    h_stage[:, cols] = (xs[:, cols] * rs * g_ref[:, cols]).astype(h_stage.dtype)

    for tp, (ext_w, ext_r) in enumerate(((ext_a, ext_b), (ext_b, ext_a))):
        for hp, h_cur in enumerate((h_a, h_b)):
            pl.when((t % 2 == tp) & (cur == hp))(functools.partial(stage, ext_w, ext_r, h_cur))

    rows = pl.ds(pl.multiple_of(jn * NORM_ROWS, NORM_ROWS), NORM_ROWS)
    fresh = tmm % nj < tm // NORM_ROWS
    for hp, h_next in enumerate((h_b, h_a)):
        @pl.when((cur == hp) & fresh)
        def _(h_next=h_next):
            h_next[rows, :] = h_stage[...]


def _upproj(x, g, w, conv_p, w_down, layer, seq, tm=1024, tn=256):
    m, k = x.shape
    assert seq % tm == 0 and m % seq == 0 and D_FF % tn == 0 and k % (K_SPLIT * 256) == 0
    nj = D_FF // tn
    n_tiles = (m // tm) * nj
    n_down = w_down.shape[2]
    slab = D_FF // n_tiles
    assert slab * n_tiles == D_FF and slab % 16 == 0
    slabs_per_block = tm // NORM_ROWS
    assert slabs_per_block <= nj
    n_slabs = m // NORM_ROWS
    h0 = _rmsnorm(x, g, BF16, rows=tm)
    body = functools.partial(_up_body, tm=tm, nj=nj, n_tiles=n_tiles, blocks_per_seq=seq // tm)
    mm = lambda t: jnp.minimum(t, n_tiles - 1)
    ep = lambda t: jnp.maximum(t - 1, 0)

    def next_slab(t):
        s = (mm(t) // nj + 1) * slabs_per_block + jnp.minimum(mm(t) % nj, slabs_per_block - 1)
        return (jnp.minimum(s, n_slabs - 1), 0)

    return pl.pallas_call(
        body,
        grid=(n_tiles + 1,),
        in_specs=[pl.BlockSpec((NORM_ROWS, k), next_slab),
                  pl.BlockSpec((1, k), lambda t: (0, 0)),
                  pl.BlockSpec((tm, k), lambda t: (0, 0), pipeline_mode=pl.Buffered(1)),
                  pl.BlockSpec((None, k, tn), lambda t: (layer, 0, mm(t) % nj)),
                  pl.BlockSpec((None, k, tn), lambda t: (layer, 0, mm(t) % nj + nj)),
                  pl.BlockSpec((None, CONV_WIDTH + 1, tn), lambda t: (layer, 0, ep(t) % nj)),
                  pl.BlockSpec((None, CONV_WIDTH + 1, tn), lambda t: (layer, 0, ep(t) % nj + nj)),
                  pl.BlockSpec((None, slab, n_down), lambda t: (layer, mm(t), 0))],
        out_specs=[pl.BlockSpec((tm, tn), lambda t: (ep(t) // nj, ep(t) % nj)),
                   pl.BlockSpec((slab, n_down), lambda t: (mm(t), 0))],
        out_shape=[jax.ShapeDtypeStruct((m, D_FF), BF16),
                   jax.ShapeDtypeStruct((D_FF, n_down), BF16)],
        scratch_shapes=[pltpu.VMEM((tm, k), BF16),
                        pltpu.VMEM((tm, k), BF16),
                        pltpu.VMEM((NORM_ROWS, k), BF16),
                        pltpu.VMEM((2, tm + CARRY_ROWS, tn), F32),
                        pltpu.VMEM((2, tm + CARRY_ROWS, tn), F32),
                        pltpu.VMEM((2 * nj, CARRY_ROWS, tn), F32)],
        compiler_params=_params("arbitrary"),
        name="upproj_conv_gate",
    )(x, g.reshape(1, k), h0, w, w, conv_p, conv_p, w_down)


def _down_body(a_ref, w_ref, x_ref, o_ref):
    o_ref[...] = x_ref[...] + jnp.dot(a_ref[...], w_ref[...], preferred_element_type=F32)


def _downproj(a, w, x, tm=1024, tn=256):
    m, k = a.shape
    n = w.shape[1]
    return pl.pallas_call(
        _down_body,
        grid=(m // tm, n // tn),
        in_specs=[pl.BlockSpec((tm, k), lambda i, j: (i, 0)),
                  pl.BlockSpec((k, tn), lambda i, j: (0, j)),
                  pl.BlockSpec((tm, tn), lambda i, j: (i, j))],
        out_specs=pl.BlockSpec((tm, tn), lambda i, j: (i, j)),
        out_shape=jax.ShapeDtypeStruct((m, n), F32),
        compiler_params=_params("parallel", "arbitrary"),
        name="downproj",
    )(a, w, x)


def kernel(x, attn_norm_g, w_in, sgu_norm_g, w_spatial, b_spatial, mix_norm_g, w_out, ffn_norm_g,
           w_up, conv_w, conv_b, w_down, final_norm_g):
    b, s, d = x.shape
    depth = w_in.shape[0]
    xm = x.reshape(b * s, d)
    slopes = 2.0 ** (-8.0 * jnp.arange(1, N_HEADS + 1, dtype=F32) / N_HEADS)
    slopes = jnp.broadcast_to(slopes[:, None, None], (N_HEADS, 1, HEAD_DIM))
    conv_p = jnp.concatenate([conv_w, conv_b[:, None, :]], axis=1)
    for i in range(depth):
        mix_g = mix_norm_g[i].reshape(1, D_MODEL)
        h = _rmsnorm(xm, attn_norm_g[i], BF16)
        z = _inproj(h, w_in, i)
        a = _attention(z.reshape(b, s, N_IN), slopes, mix_g).reshape(b * s, D_ATTN)
        g, w_out_bf = _sgu(z, sgu_norm_g[i], w_spatial[i], b_spatial[i].T, mix_g, w_out, i)
        xm = _outproj(a, g, w_out_bf, xm)
        act, w_down_bf = _upproj(xm, ffn_norm_g[i], w_up, conv_p, w_down, i, s)
        xm = _downproj(act, w_down_bf, xm)
    return _rmsnorm(xm, final_norm_g, F32).reshape(b, s, d)
```

```python
import functools
import math

import jax
import jax.numpy as jnp
from jax import lax
from jax.experimental import pallas as pl
from jax.experimental.pallas import tpu as pltpu

F32 = jnp.float32
BF16 = jnp.bfloat16

D_MODEL = 4096
HEAD_DIM = 128
D_ATTN = D_MODEL // 2
D_SGU = D_MODEL - D_ATTN
N_HEADS = D_ATTN // HEAD_DIM
N_GROUPS = D_SGU // HEAD_DIM
CHUNK = 128
BLOCK = 128
DILATIONS = (1, 4, 16)
N_IN = 3 * D_ATTN + 2 * D_SGU
D_FF = 256 * ((8 * D_MODEL // 3 + 255) // 256)
CONV_WIDTH = 3
EPS = 1e-6
LOG2E = math.log2(math.e)
VMEM_LIMIT = 61 * 1024 * 1024
CARRY_ROWS = 8
EPI_ROWS = 32
K_SPLIT = 16
NORM_ROWS = 32
SGU_BATCH = 4
ATT_GROUP = 32


def _params(*sem):
    return pltpu.CompilerParams(dimension_semantics=sem, vmem_limit_bytes=VMEM_LIMIT)


LOOKAHEAD = pl.Buffered(2, use_lookahead=True)


def _pipelined_call(body, grid, in_specs, out_spec, out_shape, name, *args):
    def outer(*refs):
        pltpu.emit_pipeline(body, grid=grid, in_specs=in_specs, out_specs=[out_spec])(*refs)

    return pl.pallas_call(
        outer,
        in_specs=[pl.BlockSpec(memory_space=pl.ANY)] * len(args),
        out_specs=pl.BlockSpec(memory_space=pl.ANY),
        out_shape=out_shape,
        compiler_params=pltpu.CompilerParams(vmem_limit_bytes=VMEM_LIMIT),
        name=name,
    )(*args)


def _rmsnorm_body(x_ref, g_ref, o_ref):
    x = x_ref[...]
    y = x * lax.rsqrt(jnp.mean(x * x, axis=-1, keepdims=True) + EPS)
    o_ref[...] = (y * g_ref[...]).astype(o_ref.dtype)


def _rmsnorm(x, g, out_dtype, tm=512, rows=None):
    m, d = x.shape
    m = m if rows is None else rows
    return pl.pallas_call(
        _rmsnorm_body,
        grid=(m // tm,),
        in_specs=[pl.BlockSpec((tm, d), lambda i: (i, 0)),
                  pl.BlockSpec((1, d), lambda i: (0, 0))],
        out_specs=pl.BlockSpec((tm, d), lambda i: (i, 0)),
        out_shape=jax.ShapeDtypeStruct((m, d), out_dtype),
        compiler_params=_params("parallel"),
        name="rmsnorm",
    )(x, g.reshape(1, d))


def _inproj_body(h_ref, w_ref, o_ref):
    w = w_ref[...].astype(BF16)
    o_ref[...] = jnp.dot(h_ref[...], w, preferred_element_type=F32).astype(o_ref.dtype)


def _inproj(h, w, layer, tm=1024, tn=1024):
    m, k = h.shape
    n = w.shape[2]
    return _pipelined_call(
        _inproj_body, (m // tm, n // tn),
        [pl.BlockSpec((tm, k), lambda i, j: (i, 0), pipeline_mode=LOOKAHEAD),
         pl.BlockSpec((None, k, tn), lambda i, j: (layer, 0, j))],
        pl.BlockSpec((tm, tn), lambda i, j: (i, j)),
        jax.ShapeDtypeStruct((m, n), BF16), "inproj", h, w)


def _aligned(i, m):
    return i if isinstance(i, int) else pl.multiple_of(i, m)


def _attn_body(q_ref, k_ref, v_ref, slope_ref, g_ref, o_ref, xf, x4f, qd, kd, vd, bias2, ob, mb, lb,
               *, seq):
    qscale = HEAD_DIM ** -0.5 * LOG2E
    d1, d2 = DILATIONS[1], DILATIONS[2]
    assert d2 == d1 * d1 and DILATIONS[0] == 1
    sub1, sub2 = seq // d1, seq // d2
    for src, dst, mul, first_c in ((q_ref, qd, qscale, 0), (k_ref, kd, None, 1), (v_ref, vd, None, 1)):
        if first_c == 0:
            xf[...] = src[...].astype(F32) * mul
            dst[0] = xf[...].astype(BF16)
        else:
            xf[...] = src[...].astype(F32)

        def deint1(r, carry, dst=dst, slot=1 - first_c):
            rows = pl.ds(pl.multiple_of(r * sub1, sub1), sub1)
            x = xf[pl.ds(r, sub1, stride=d1), :]
            x4f[rows, :] = x
            dst[slot, rows, :] = x.astype(BF16)
            return carry

        lax.fori_loop(0, d1, deint1, 0)

        def deint2(r, carry, dst=dst, slot=2 - first_c):
            rows = pl.ds(pl.multiple_of(r * sub2, sub2), sub2)
            x = x4f[pl.ds((r % d1) * sub1 + r // d1, sub2, stride=d1), :]
            dst[slot, rows, :] = x.astype(BF16)
            return carry

        lax.fori_loop(0, d2, deint2, 0)

    slope = slope_ref[...]
    qi = lax.broadcasted_iota(jnp.int32, (BLOCK, 2 * BLOCK), 0)
    kj = lax.broadcasted_iota(jnp.int32, (BLOCK, 2 * BLOCK), 1)
    dist = qi + BLOCK - kj
    valid = (dist >= 0) & (dist <= BLOCK)
    for c, d in enumerate(DILATIONS):
        bias2[c] = jnp.where(valid, (-LOG2E * slope[:, :1]) * (dist * d).astype(F32), -jnp.inf)
    ones = jnp.ones((2 * BLOCK, HEAD_DIM), BF16)

    def group(c, d, blocks):
        getk = (lambda rows: k_ref[rows, :]) if c == 0 else (lambda rows: kd[c - 1, rows, :])
        getv = (lambda rows: v_ref[rows, :]) if c == 0 else (lambda rows: vd[c - 1, rows, :])
        qs, ks, vs = [], [], []
        for gi, _, first in blocks:
            qrows = pl.ds(_aligned(gi * BLOCK, BLOCK), BLOCK)
            krows = qrows if first else pl.ds(_aligned(gi * BLOCK - BLOCK, BLOCK), 2 * BLOCK)
            qs.append(qd[c, qrows, :])
            ks.append(getk(krows))
            v = getv(krows)
            vs.append(jnp.concatenate([v, ones[:v.shape[0]]], axis=1))
        ss = [lax.dot_general(q, k, (((1,), (1,)), ((), ())), preferred_element_type=F32)
              for q, k in zip(qs, ks)]
        ps, ms = [], []
        for s, (_, _, first) in zip(ss, blocks):
            s = s + (bias2[c, :, BLOCK:] if first else bias2[c])
            mx = jnp.max(s, axis=-1, keepdims=True)
            ms.append(mx)
            ps.append(jnp.exp2(s - mx).astype(BF16))
        os = [jnp.dot(p, v, preferred_element_type=F32) for p, v in zip(ps, vs)]
        for o, mx, (_, rows, _) in zip(os, ms, blocks):
            ob[c, rows, :] = o[:, :HEAD_DIM]
            lb[c, rows, :] = o[:, HEAD_DIM:]
            mb[c, rows, :] = jnp.broadcast_to(mx, (BLOCK, HEAD_DIM))

    for c, d in enumerate(DILATIONS):
        nb = seq // d // BLOCK

        def blk(r, n, first, d=d, nb=nb):
            gi = r * nb + n
            if d <= d1:
                return (gi, pl.ds(_aligned(gi * BLOCK, BLOCK), BLOCK), first)
            q = d // d1
            start = (r % d1) * sub1 + q * BLOCK * n + r // d1
            return (gi, pl.ds(start, BLOCK, stride=q), first)

        if nb >= ATT_GROUP:
            def r_body(r, carry, c=c, d=d, nb=nb, blk=blk):
                group(c, d, [blk(r, n, n == 0) for n in range(ATT_GROUP)])

                def n_body(i, carry2):
                    group(c, d, [blk(r, i * ATT_GROUP + t, False) for t in range(ATT_GROUP)])
                    return carry2

                return lax.fori_loop(1, nb // ATT_GROUP, n_body, carry)

            if d == 1:
                r_body(0, 0)
            else:
                lax.fori_loop(0, d, r_body, 0)
        else:
            per = ATT_GROUP // nb

            def r_body(i, carry, c=c, d=d, nb=nb, blk=blk, per=per):
                group(c, d, [blk(i * per + t, n, n == 0) for t in range(per) for n in range(nb)])
                return carry

            lax.fori_loop(0, d // per, r_body, 0)

    g = g_ref[...]
    step, piece = 512, 64

    def fin(t, carry):
        r = (t * step) // sub1
        m0 = (t * step) % sub1
        for p0 in range(0, step, piece):
            rows = pl.ds(pl.multiple_of(t * step + p0, piece), piece)
            toks = pl.ds(r + d1 * (m0 + p0), piece, stride=d1)
            ma, mb1, mb2 = mb[0, toks, :], mb[1, rows, :], mb[2, rows, :]
            mx = jnp.maximum(jnp.maximum(ma, mb1), mb2)
            e0, e1, e2 = jnp.exp2(ma - mx), jnp.exp2(mb1 - mx), jnp.exp2(mb2 - mx)
            den = e0 * lb[0, toks, :] + e1 * lb[1, rows, :] + e2 * lb[2, rows, :]
            num = e0 * ob[0, toks, :] + e1 * ob[1, rows, :] + e2 * ob[2, rows, :]
            msq = jnp.mean(num * num, axis=-1, keepdims=True)
            xf[toks, :] = num * lax.rsqrt(msq + EPS * (den * den)) * g
        return carry

    lax.fori_loop(0, seq // step, fin, 0)
    o_ref[...] = xf[...].astype(o_ref.dtype)


def _attention(z3, slopes, mix_g):
    b, s, _ = z3.shape
    blk = lambda off: pl.BlockSpec((None, s, HEAD_DIM), lambda bi, h, off=off: (bi, 0, off + h))
    return pl.pallas_call(
        functools.partial(_attn_body, seq=s),
        grid=(b, N_HEADS),
        in_specs=[blk(0), blk(N_HEADS), blk(2 * N_HEADS),
                  pl.BlockSpec((None, 1, HEAD_DIM), lambda bi, h: (h, 0, 0)),
                  pl.BlockSpec((1, HEAD_DIM), lambda bi, h: (0, h))],
        out_specs=pl.BlockSpec((None, s, HEAD_DIM), lambda bi, h: (bi, 0, h)),
        out_shape=jax.ShapeDtypeStruct((b, s, D_ATTN), BF16),
        scratch_shapes=[pltpu.VMEM((s, HEAD_DIM), F32)] * 2
        + [pltpu.VMEM((len(DILATIONS), s, HEAD_DIM), BF16)]
        + [pltpu.VMEM((len(DILATIONS) - 1, s, HEAD_DIM), BF16)] * 2
        + [pltpu.VMEM((len(DILATIONS), BLOCK, 2 * BLOCK), F32)]
        + [pltpu.VMEM((len(DILATIONS), s, HEAD_DIM), F32)] * 3,
        compiler_params=_params("parallel", "parallel"),
        name="dilated_attn",
    )(z3, z3, z3, slopes, mix_g)


def _gelu(x):
    return 0.5 * x * (1.0 + lax.erf(x * math.sqrt(0.5)))


def _sgu_body(u_ref, v_ref, ng_ref, w_ref, bt_ref, mg_ref, wo_ref, o_ref, wob_ref, *, tm):
    wob_ref[...] = wo_ref[...].astype(wob_ref.dtype)
    ti = lax.broadcasted_iota(jnp.int32, (CHUNK, CHUNK), 0)
    si = lax.broadcasted_iota(jnp.int32, (CHUNK, CHUNK), 1)
    causal = si <= ti
    n_ch = tm // CHUNK
    for g0 in range(0, N_GROUPS, SGU_BATCH):
        groups = range(g0, g0 + SGU_BATCH)
        vcat = []
        for g in groups:
            cols = slice(g * HEAD_DIM, (g + 1) * HEAD_DIM)
            ng = ng_ref[:, cols]
            vs = []
            for ch in range(n_ch):
                v = _gelu(v_ref[ch * CHUNK:(ch + 1) * CHUNK, cols].astype(F32))
                v = v * lax.rsqrt(jnp.mean(v * v, axis=-1, keepdims=True) + EPS) * ng
                vs.append(v.astype(BF16))
            vcat.append(jnp.concatenate(vs, axis=1))
        mixed = [jnp.dot(jnp.where(causal, w_ref[g], 0.0).astype(BF16), vc, preferred_element_type=F32)
                 for g, vc in zip(groups, vcat)]
        for g, mx in zip(groups, mixed):
            cols = slice(g * HEAD_DIM, (g + 1) * HEAD_DIM)
            bcol = bt_ref[:, g:g + 1]
            mg = mg_ref[:, cols]
            for ch in range(n_ch):
                rows = slice(ch * CHUNK, (ch + 1) * CHUNK)
                u = _gelu(u_ref[rows, cols].astype(F32))
                y = u * (mx[:, ch * HEAD_DIM:(ch + 1) * HEAD_DIM] + bcol)
                y = y * lax.rsqrt(jnp.mean(y * y, axis=-1, keepdims=True) + EPS) * mg
                o_ref[rows, cols] = y.astype(o_ref.dtype)


def _sgu(z, sgu_norm_g, w_spatial, b_spatial_t, mix_g, w_out, layer, tm=256):
    m = z.shape[0]
    steps = m // tm
    k_out, n_out = w_out.shape[1:]
    slab = k_out // steps
    assert slab * steps == k_out and slab % 16 == 0
    ublk = 3 * D_ATTN // D_SGU
    return pl.pallas_call(
        functools.partial(_sgu_body, tm=tm),
        grid=(steps,),
        in_specs=[pl.BlockSpec((tm, D_SGU), lambda i: (i, ublk)),
                  pl.BlockSpec((tm, D_SGU), lambda i: (i, ublk + 1)),
                  pl.BlockSpec((1, D_SGU), lambda i: (0, 0)),
                  pl.BlockSpec((N_GROUPS, CHUNK, CHUNK), lambda i: (0, 0, 0)),
                  pl.BlockSpec((CHUNK, N_GROUPS), lambda i: (0, 0)),
                  pl.BlockSpec((1, D_SGU), lambda i: (0, 1)),
                  pl.BlockSpec((None, slab, n_out), lambda i: (layer, i, 0))],
        out_specs=[pl.BlockSpec((tm, D_SGU), lambda i: (i, 0)),
                   pl.BlockSpec((slab, n_out), lambda i: (i, 0))],
        out_shape=[jax.ShapeDtypeStruct((m, D_SGU), BF16),
                   jax.ShapeDtypeStruct((k_out, n_out), BF16)],
        compiler_params=_params("parallel"),
        name="sgu",
    )(z, z, sgu_norm_g.reshape(1, D_SGU), w_spatial, b_spatial_t, mix_g, w_out)


def _outproj_body(a_ref, g_ref, wa_ref, wg_ref, x_ref, o_ref):
    acc = jnp.dot(a_ref[...], wa_ref[...], preferred_element_type=F32)
    acc = acc + jnp.dot(g_ref[...], wg_ref[...], preferred_element_type=F32)
    o_ref[...] = x_ref[...] + acc


def _outproj(a, g, w, x, tm=1024, tn=1024):
    m, ka = a.shape
    kg = g.shape[1]
    assert ka == kg
    n = w.shape[1]
    return _pipelined_call(
        _outproj_body, (m // tm, n // tn),
        [pl.BlockSpec((tm, ka), lambda i, j: (i, 0), pipeline_mode=LOOKAHEAD),
         pl.BlockSpec((tm, kg), lambda i, j: (i, 0), pipeline_mode=LOOKAHEAD),
         pl.BlockSpec((ka, tn), lambda i, j: (0, j)),
         pl.BlockSpec((kg, tn), lambda i, j: (1, j)),
         pl.BlockSpec((tm, tn), lambda i, j: (i, j))],
        pl.BlockSpec((tm, tn), lambda i, j: (i, j)),
        jax.ShapeDtypeStruct((m, n), F32), "outproj", a, g, w, w, x)


def _up_body(x_ref, g_ref, h0_ref, wg_ref, wv_ref, cwg_ref, cwv_ref, cbg_ref, cbv_ref, wd_ref,
             o_ref, wdo_ref, h_a, h_b, h_stage, ext_a, ext_b, carry_ref, *, tm, nj, n_tiles, blocks_per_seq):
    t = pl.program_id(0)
    te = jnp.maximum(t - 1, 0)
    je = te % nj
    seq_start = ((te // nj) % blocks_per_seq) == 0
    tmm = jnp.minimum(t, n_tiles - 1)
    cur = (tmm // nj) % 2
    jn = jnp.minimum(tmm % nj, tm // NORM_ROWS - 1)

    @pl.when(t == 0)
    def _():
        ext_a[...] = jnp.zeros_like(ext_a)
        ext_b[...] = jnp.zeros_like(ext_b)
        carry_ref[...] = jnp.zeros_like(carry_ref)
        h_a[...] = h0_ref[...]

    def stage(ext_w, ext_r, h_cur):
        for half in range(2):
            slot = half * nj + je
            prev = jnp.where(seq_start, 0.0, carry_ref[slot])
            carry_ref[slot] = ext_r[half, tm:tm + CARRY_ROWS, :]
            ext_r[half, 0:CARRY_ROWS, :] = prev

        def conv(half, r0, cw_ref, cb_ref):
            slab = ext_r[half, r0:r0 + CARRY_ROWS + EPI_ROWS, :]
            y = cb_ref[...]
            for j in range(CONV_WIDTH - 1, 0, -1):
                tap = pltpu.roll(slab, j, axis=0)[CARRY_ROWS:, :]
                y = y + cw_ref[CONV_WIDTH - 1 - j:CONV_WIDTH - j, :] * tap
            return y + cw_ref[CONV_WIDTH - 1:CONV_WIDTH, :] * slab[CARRY_ROWS:, :]

        kc = h_cur.shape[1] // K_SPLIT
        rows_per_piece = tm // (2 * K_SPLIT)
        piece = 0
        for half, w_ref in enumerate((wg_ref, wv_ref)):
            acc = None
            for p in range(K_SPLIT):
                ks = slice(p * kc, (p + 1) * kc)
                part = jnp.dot(h_cur[:, ks], w_ref[ks, :].astype(BF16), preferred_element_type=F32)
                acc = part if acc is None else acc + part
                for r0 in range(piece * rows_per_piece, (piece + 1) * rows_per_piece, EPI_ROWS):
                    gate = conv(0, r0, cwg_ref, cbg_ref)
                    val = conv(1, r0, cwv_ref, cbv_ref)
                    o_ref[r0:r0 + EPI_ROWS, :] = (jax.nn.silu(gate) * val).astype(o_ref.dtype)
                piece += 1
                if piece == K_SPLIT // 2:
                    wdo_ref[...] = wd_ref[...].astype(wdo_ref.dtype)
            ext_w[half, CARRY_ROWS:, :] = acc
            if half == 0:
                xs = x_ref[...]
                width = xs.shape[1]
                part = None
                for c0 in range(0, width, HEAD_DIM):
                    sq = xs[:, c0:c0 + HEAD_DIM] * xs[:, c0:c0 + HEAD_DIM]
                    part = sq if part is None else part + sq
                hi = part.astype(BF16)
                lo = (part - hi.astype(F32)).astype(BF16)
                ones = jnp.ones((HEAD_DIM, HEAD_DIM), BF16)
                ssq = (jnp.dot(hi, ones, preferred_element_type=F32)
                       + jnp.dot(lo, ones, preferred_element_type=F32))
                rs = lax.rsqrt(ssq * (1.0 / width) + EPS)
                for c0 in range(0, width, HEAD_DIM):
                    cols = slice(c0, c0 + HEAD_DIM)
                    h_stage[:, cols] = (xs[:, cols] * rs * g_ref[:, cols]).astype(h_stage.dtype)

    for tp, (ext_w, ext_r) in enumerate(((ext_a, ext_b), (ext_b, ext_a))):
        for hp, h_cur in enumerate((h_a, h_b)):
            pl.when((t % 2 == tp) & (cur == hp))(functools.partial(stage, ext_w, ext_r, h_cur))

    rows = pl.ds(pl.multiple_of(jn * NORM_ROWS, NORM_ROWS), NORM_ROWS)
    for hp, h_next in enumerate((h_b, h_a)):
        @pl.when(cur == hp)
        def _(h_next=h_next):
            h_next[rows, :] = h_stage[...]


def _upproj(x, g, w, conv_w, conv_b, w_down, layer, seq, tm=1024, tn=256):
    m, k = x.shape
    nj = D_FF // tn
    n_tiles = (m // tm) * nj
    n_down = w_down.shape[2]
    slab = D_FF // n_tiles
    assert slab * n_tiles == D_FF and slab % 16 == 0
    slabs_per_block = tm // NORM_ROWS
    assert slabs_per_block <= nj
    n_slabs = m // NORM_ROWS
    h0 = _rmsnorm(x, g, BF16, rows=tm)
    body = functools.partial(_up_body, tm=tm, nj=nj, n_tiles=n_tiles, blocks_per_seq=seq // tm)
    mm = lambda t: jnp.minimum(t, n_tiles - 1)
    ep = lambda t: jnp.maximum(t - 1, 0)

    def next_slab(t):
        s = (mm(t) // nj + 1) * slabs_per_block + jnp.minimum(mm(t) % nj, slabs_per_block - 1)
        return (jnp.minimum(s, n_slabs - 1), 0)

    return pl.pallas_call(
        body,
        grid=(n_tiles + 1,),
        in_specs=[pl.BlockSpec((NORM_ROWS, k), next_slab),
                  pl.BlockSpec((1, k), lambda t: (0, 0)),
                  pl.BlockSpec((tm, k), lambda t: (0, 0), pipeline_mode=pl.Buffered(1)),
                  pl.BlockSpec((None, k, tn), lambda t: (layer, 0, mm(t) % nj)),
                  pl.BlockSpec((None, k, tn), lambda t: (layer, 0, mm(t) % nj + nj)),
                  pl.BlockSpec((None, CONV_WIDTH, tn), lambda t: (layer, 0, ep(t) % nj)),
                  pl.BlockSpec((None, CONV_WIDTH, tn), lambda t: (layer, 0, ep(t) % nj + nj)),
                  pl.BlockSpec((None, 1, tn), lambda t: (layer, 0, ep(t) % nj)),
                  pl.BlockSpec((None, 1, tn), lambda t: (layer, 0, ep(t) % nj + nj)),
                  pl.BlockSpec((None, slab, n_down), lambda t: (layer, mm(t), 0))],
        out_specs=[pl.BlockSpec((tm, tn), lambda t: (ep(t) // nj, ep(t) % nj)),
                   pl.BlockSpec((slab, n_down), lambda t: (mm(t), 0))],
        out_shape=[jax.ShapeDtypeStruct((m, D_FF), BF16),
                   jax.ShapeDtypeStruct((D_FF, n_down), BF16)],
        scratch_shapes=[pltpu.VMEM((tm, k), BF16),
                        pltpu.VMEM((tm, k), BF16),
                        pltpu.VMEM((NORM_ROWS, k), BF16),
                        pltpu.VMEM((2, tm + CARRY_ROWS, tn), F32),
                        pltpu.VMEM((2, tm + CARRY_ROWS, tn), F32),
                        pltpu.VMEM((2 * nj, CARRY_ROWS, tn), F32)],
        compiler_params=_params("arbitrary"),
        name="upproj_conv_gate",
    )(x, g.reshape(1, k), h0, w, w, conv_w, conv_w, conv_b, conv_b, w_down)


def _down_body(a_ref, w_ref, x_ref, o_ref):
    o_ref[...] = x_ref[...] + jnp.dot(a_ref[...], w_ref[...], preferred_element_type=F32)


def _downproj(a, w, x, tm=512, tn=512):
    m, k = a.shape
    n = w.shape[1]
    return _pipelined_call(
        _down_body, (m // tm, n // tn),
        [pl.BlockSpec((tm, k), lambda i, j: (i, 0), pipeline_mode=LOOKAHEAD),
         pl.BlockSpec((k, tn), lambda i, j: (0, j)),
         pl.BlockSpec((tm, tn), lambda i, j: (i, j))],
        pl.BlockSpec((tm, tn), lambda i, j: (i, j)),
        jax.ShapeDtypeStruct((m, n), F32), "downproj", a, w, x)


def kernel(x, attn_norm_g, w_in, sgu_norm_g, w_spatial, b_spatial, mix_norm_g, w_out, ffn_norm_g,
           w_up, conv_w, conv_b, w_down, final_norm_g):
    b, s, d = x.shape
    depth = w_in.shape[0]
    xm = x.reshape(b * s, d)
    slopes = 2.0 ** (-8.0 * jnp.arange(1, N_HEADS + 1, dtype=F32) / N_HEADS)
    slopes = jnp.broadcast_to(slopes[:, None, None], (N_HEADS, 1, HEAD_DIM))
    conv_b3 = conv_b.reshape(depth, 1, 2 * D_FF)
    for i in range(depth):
        mix_g = mix_norm_g[i].reshape(1, D_MODEL)
        h = _rmsnorm(xm, attn_norm_g[i], BF16)
        z = _inproj(h, w_in, i)
        a = _attention(z.reshape(b, s, N_IN), slopes, mix_g).reshape(b * s, D_ATTN)
        g, w_out_bf = _sgu(z, sgu_norm_g[i], w_spatial[i], b_spatial[i].T, mix_g, w_out, i)
        xm = _outproj(a, g, w_out_bf, xm)
        act, w_down_bf = _upproj(xm, ffn_norm_g[i], w_up, conv_w, conv_b3, w_down, i, s)
        xm = _downproj(act, w_down_bf, xm)
    return _rmsnorm(xm, final_norm_g, F32).reshape(b, s, d)
```

```python
import functools
import math

import jax
import jax.numpy as jnp
from jax import lax
from jax.experimental import pallas as pl
from jax.experimental.pallas import tpu as pltpu

F32 = jnp.float32
BF16 = jnp.bfloat16

D_MODEL = 4096
HEAD_DIM = 128
D_ATTN = D_MODEL // 2
D_SGU = D_MODEL - D_ATTN
N_HEADS = D_ATTN // HEAD_DIM
N_GROUPS = D_SGU // HEAD_DIM
CHUNK = 128
BLOCK = 128
DILATIONS = (1, 4, 16)
N_IN = 3 * D_ATTN + 2 * D_SGU
D_FF = 256 * ((8 * D_MODEL // 3 + 255) // 256)
CONV_WIDTH = 3
EPS = 1e-6
LOG2E = math.log2(math.e)
VMEM_LIMIT = 61 * 1024 * 1024
CARRY_ROWS = 8
EPI_ROWS = 32
K_SPLIT = 16
LAST_CHAIN_ROWS = 256
NORM_ROWS = 32
SGU_BATCH = 4
ATT_GROUP = 32


def _params(*sem):
    return pltpu.CompilerParams(dimension_semantics=sem, vmem_limit_bytes=VMEM_LIMIT)


def _rmsnorm_body(x_ref, g_ref, o_ref):
    x = x_ref[...]
    y = x * lax.rsqrt(jnp.mean(x * x, axis=-1, keepdims=True) + EPS)
    o_ref[...] = (y * g_ref[...]).astype(o_ref.dtype)


def _rmsnorm(x, g, out_dtype, tm=512, rows=None):
    m, d = x.shape
    m = m if rows is None else rows
    return pl.pallas_call(
        _rmsnorm_body,
        grid=(m // tm,),
        in_specs=[pl.BlockSpec((tm, d), lambda i: (i, 0)),
                  pl.BlockSpec((1, d), lambda i: (0, 0))],
        out_specs=pl.BlockSpec((tm, d), lambda i: (i, 0)),
        out_shape=jax.ShapeDtypeStruct((m, d), out_dtype),
        compiler_params=_params("parallel"),
        name="rmsnorm",
    )(x, g.reshape(1, d))


def _inproj_body(h_ref, w_ref, o_ref):
    w = w_ref[...].astype(BF16)
    o_ref[...] = jnp.dot(h_ref[...], w, preferred_element_type=F32).astype(o_ref.dtype)


def _inproj(h, w, layer, tm=1024, tn=1024):
    m, k = h.shape
    n = w.shape[2]
    return pl.pallas_call(
        _inproj_body,
        grid=(m // tm, n // tn),
        in_specs=[pl.BlockSpec((tm, k), lambda i, j: (i, 0)),
                  pl.BlockSpec((None, k, tn), lambda i, j: (layer, 0, j))],
        out_specs=pl.BlockSpec((tm, tn), lambda i, j: (i, j)),
        out_shape=jax.ShapeDtypeStruct((m, n), BF16),
        compiler_params=_params("parallel", "arbitrary"),
        name="inproj",
    )(h, w)


def _aligned(i, m):
    return i if isinstance(i, int) else pl.multiple_of(i, m)


def _attn_body(q_ref, k_ref, v_ref, slope_ref, g_ref, o_ref, xf, x4f, qd, kd, vd, bias2, ob, mb, lb,
               *, seq):
    qscale = HEAD_DIM ** -0.5 * LOG2E
    d1, d2 = DILATIONS[1], DILATIONS[2]
    assert d2 == d1 * d1 and DILATIONS[0] == 1
    sub1, sub2 = seq // d1, seq // d2
    for src, dst, mul, first_c in ((q_ref, qd, qscale, 0), (k_ref, kd, None, 1), (v_ref, vd, None, 1)):
        if first_c == 0:
            xf[...] = src[...].astype(F32) * mul
            dst[0] = xf[...].astype(BF16)
        else:
            xf[...] = src[...].astype(F32)

        def deint1(r, carry, dst=dst, slot=1 - first_c):
            rows = pl.ds(pl.multiple_of(r * sub1, sub1), sub1)
            x = xf[pl.ds(r, sub1, stride=d1), :]
            x4f[rows, :] = x
            dst[slot, rows, :] = x.astype(BF16)
            return carry

        lax.fori_loop(0, d1, deint1, 0)

        def deint2(r, carry, dst=dst, slot=2 - first_c):
            rows = pl.ds(pl.multiple_of(r * sub2, sub2), sub2)
            x = x4f[pl.ds((r % d1) * sub1 + r // d1, sub2, stride=d1), :]
            dst[slot, rows, :] = x.astype(BF16)
            return carry

        lax.fori_loop(0, d2, deint2, 0)

    slope = slope_ref[...]
    qi = lax.broadcasted_iota(jnp.int32, (BLOCK, 2 * BLOCK), 0)
    kj = lax.broadcasted_iota(jnp.int32, (BLOCK, 2 * BLOCK), 1)
    dist = qi + BLOCK - kj
    valid = (dist >= 0) & (dist <= BLOCK)
    for c, d in enumerate(DILATIONS):
        bias2[c] = jnp.where(valid, (-LOG2E * slope[:, :1]) * (dist * d).astype(F32), -jnp.inf)
    ones = jnp.ones((2 * BLOCK, HEAD_DIM), BF16)

    def group(c, d, blocks):
        getk = (lambda rows: k_ref[rows, :]) if c == 0 else (lambda rows: kd[c - 1, rows, :])
        getv = (lambda rows: v_ref[rows, :]) if c == 0 else (lambda rows: vd[c - 1, rows, :])
        qs, ks, vs = [], [], []
        for gi, _, first in blocks:
            qrows = pl.ds(_aligned(gi * BLOCK, BLOCK), BLOCK)
            krows = qrows if first else pl.ds(_aligned(gi * BLOCK - BLOCK, BLOCK), 2 * BLOCK)
            qs.append(qd[c, qrows, :])
            ks.append(getk(krows))
            v = getv(krows)
            vs.append(jnp.concatenate([v, ones[:v.shape[0]]], axis=1))
        ss = [lax.dot_general(q, k, (((1,), (1,)), ((), ())), preferred_element_type=F32)
              for q, k in zip(qs, ks)]
        ps, ms = [], []
        for s, (_, _, first) in zip(ss, blocks):
            s = s + (bias2[c, :, BLOCK:] if first else bias2[c])
            mx = jnp.max(s, axis=-1, keepdims=True)
            ms.append(mx)
            ps.append(jnp.exp2(s - mx).astype(BF16))
        os = [jnp.dot(p, v, preferred_element_type=F32) for p, v in zip(ps, vs)]
        for o, mx, (_, rows, _) in zip(os, ms, blocks):
            ob[c, rows, :] = o[:, :HEAD_DIM]
            lb[c, rows, :] = o[:, HEAD_DIM:]
            mb[c, rows, :] = jnp.broadcast_to(mx, (BLOCK, HEAD_DIM))

    for c, d in enumerate(DILATIONS):
        nb = seq // d // BLOCK

        def blk(r, n, first, d=d, nb=nb):
            gi = r * nb + n
            if d <= d1:
                return (gi, pl.ds(_aligned(gi * BLOCK, BLOCK), BLOCK), first)
            q = d // d1
            start = (r % d1) * sub1 + q * BLOCK * n + r // d1
            return (gi, pl.ds(start, BLOCK, stride=q), first)

        if nb >= ATT_GROUP:
            def r_body(r, carry, c=c, d=d, nb=nb, blk=blk):
                group(c, d, [blk(r, n, n == 0) for n in range(ATT_GROUP)])

                def n_body(i, carry2):
                    group(c, d, [blk(r, i * ATT_GROUP + t, False) for t in range(ATT_GROUP)])
                    return carry2

                return lax.fori_loop(1, nb // ATT_GROUP, n_body, carry)

            if d == 1:
                r_body(0, 0)
            else:
                lax.fori_loop(0, d, r_body, 0)
        else:
            per = ATT_GROUP // nb

            def r_body(i, carry, c=c, d=d, nb=nb, blk=blk, per=per):
                group(c, d, [blk(i * per + t, n, n == 0) for t in range(per) for n in range(nb)])
                return carry

            lax.fori_loop(0, d // per, r_body, 0)

    g = g_ref[...]
    step, piece = 512, 64

    def fin(t, carry):
        r = (t * step) // sub1
        m0 = (t * step) % sub1
        for p0 in range(0, step, piece):
            rows = pl.ds(pl.multiple_of(t * step + p0, piece), piece)
            toks = pl.ds(r + d1 * (m0 + p0), piece, stride=d1)
            ma, mb1, mb2 = mb[0, toks, :], mb[1, rows, :], mb[2, rows, :]
            mx = jnp.maximum(jnp.maximum(ma, mb1), mb2)
            e0, e1, e2 = jnp.exp2(ma - mx), jnp.exp2(mb1 - mx), jnp.exp2(mb2 - mx)
            den = e0 * lb[0, toks, :] + e1 * lb[1, rows, :] + e2 * lb[2, rows, :]
            num = e0 * ob[0, toks, :] + e1 * ob[1, rows, :] + e2 * ob[2, rows, :]
            msq = jnp.mean(num * num, axis=-1, keepdims=True)
            xf[toks, :] = num * lax.rsqrt(msq + EPS * (den * den)) * g
        return carry

    lax.fori_loop(0, seq // step, fin, 0)
    o_ref[...] = xf[...].astype(o_ref.dtype)


def _attention(z3, slopes, mix_g):
    b, s, _ = z3.shape
    blk = lambda off: pl.BlockSpec((None, s, HEAD_DIM), lambda bi, h, off=off: (bi, 0, off + h))
    return pl.pallas_call(
        functools.partial(_attn_body, seq=s),
        grid=(b, N_HEADS),
        in_specs=[blk(0), blk(N_HEADS), blk(2 * N_HEADS),
                  pl.BlockSpec((None, 1, HEAD_DIM), lambda bi, h: (h, 0, 0)),
                  pl.BlockSpec((1, HEAD_DIM), lambda bi, h: (0, h))],
        out_specs=pl.BlockSpec((None, s, HEAD_DIM), lambda bi, h: (bi, 0, h)),
        out_shape=jax.ShapeDtypeStruct((b, s, D_ATTN), BF16),
        scratch_shapes=[pltpu.VMEM((s, HEAD_DIM), F32)] * 2
        + [pltpu.VMEM((len(DILATIONS), s, HEAD_DIM), BF16)]
        + [pltpu.VMEM((len(DILATIONS) - 1, s, HEAD_DIM), BF16)] * 2
        + [pltpu.VMEM((len(DILATIONS), BLOCK, 2 * BLOCK), F32)]
        + [pltpu.VMEM((len(DILATIONS), s, HEAD_DIM), F32)] * 3,
        compiler_params=_params("parallel", "parallel"),
        name="dilated_attn",
    )(z3, z3, z3, slopes, mix_g)


def _gelu(x):
    return 0.5 * x * (1.0 + lax.erf(x * math.sqrt(0.5)))


def _sgu_body(u_ref, v_ref, ng_ref, w_ref, bt_ref, mg_ref, wo_ref, o_ref, wob_ref, *, tm):
    wob_ref[...] = wo_ref[...].astype(wob_ref.dtype)
    ti = lax.broadcasted_iota(jnp.int32, (CHUNK, CHUNK), 0)
    si = lax.broadcasted_iota(jnp.int32, (CHUNK, CHUNK), 1)
    causal = si <= ti
    n_ch = tm // CHUNK
    for g0 in range(0, N_GROUPS, SGU_BATCH):
        groups = range(g0, g0 + SGU_BATCH)
        vcat = []
        for g in groups:
            cols = slice(g * HEAD_DIM, (g + 1) * HEAD_DIM)
            ng = ng_ref[:, cols]
            vs = []
            for ch in range(n_ch):
                v = _gelu(v_ref[ch * CHUNK:(ch + 1) * CHUNK, cols].astype(F32))
                v = v * lax.rsqrt(jnp.mean(v * v, axis=-1, keepdims=True) + EPS) * ng
                vs.append(v.astype(BF16))
            vcat.append(jnp.concatenate(vs, axis=1))
        mixed = [jnp.dot(jnp.where(causal, w_ref[g], 0.0).astype(BF16), vc, preferred_element_type=F32)
                 for g, vc in zip(groups, vcat)]
        for g, mx in zip(groups, mixed):
            cols = slice(g * HEAD_DIM, (g + 1) * HEAD_DIM)
            bcol = bt_ref[:, g:g + 1]
            mg = mg_ref[:, cols]
            for ch in range(n_ch):
                rows = slice(ch * CHUNK, (ch + 1) * CHUNK)
                u = _gelu(u_ref[rows, cols].astype(F32))
                y = u * (mx[:, ch * HEAD_DIM:(ch + 1) * HEAD_DIM] + bcol)
                y = y * lax.rsqrt(jnp.mean(y * y, axis=-1, keepdims=True) + EPS) * mg
                o_ref[rows, cols] = y.astype(o_ref.dtype)


def _sgu(z, sgu_norm_g, w_spatial, b_spatial_t, mix_g, w_out, layer, tm=256):
    m = z.shape[0]
    steps = m // tm
    k_out, n_out = w_out.shape[1:]
    slab = k_out // steps
    assert slab * steps == k_out and slab % 16 == 0
    ublk = 3 * D_ATTN // D_SGU
    return pl.pallas_call(
        functools.partial(_sgu_body, tm=tm),
        grid=(steps,),
        in_specs=[pl.BlockSpec((tm, D_SGU), lambda i: (i, ublk)),
                  pl.BlockSpec((tm, D_SGU), lambda i: (i, ublk + 1)),
                  pl.BlockSpec((1, D_SGU), lambda i: (0, 0)),
                  pl.BlockSpec((N_GROUPS, CHUNK, CHUNK), lambda i: (0, 0, 0)),
                  pl.BlockSpec((CHUNK, N_GROUPS), lambda i: (0, 0)),
                  pl.BlockSpec((1, D_SGU), lambda i: (0, 1)),
                  pl.BlockSpec((None, slab, n_out), lambda i: (layer, i, 0))],
        out_specs=[pl.BlockSpec((tm, D_SGU), lambda i: (i, 0)),
                   pl.BlockSpec((slab, n_out), lambda i: (i, 0))],
        out_shape=[jax.ShapeDtypeStruct((m, D_SGU), BF16),
                   jax.ShapeDtypeStruct((k_out, n_out), BF16)],
        compiler_params=_params("parallel"),
        name="sgu",
    )(z, z, sgu_norm_g.reshape(1, D_SGU), w_spatial, b_spatial_t, mix_g, w_out)


def _outproj_body(a_ref, g_ref, wa_ref, wg_ref, x_ref, o_ref):
    acc = jnp.dot(a_ref[...], wa_ref[...], preferred_element_type=F32)
    acc = acc + jnp.dot(g_ref[...], wg_ref[...], preferred_element_type=F32)
    o_ref[...] = x_ref[...] + acc


def _outproj(a, g, w, x, tm=1024, tn=1024):
    m, ka = a.shape
    kg = g.shape[1]
    assert ka == kg
    n = w.shape[1]
    return pl.pallas_call(
        _outproj_body,
        grid=(m // tm, n // tn),
        in_specs=[pl.BlockSpec((tm, ka), lambda i, j: (i, 0)),
                  pl.BlockSpec((tm, kg), lambda i, j: (i, 0)),
                  pl.BlockSpec((ka, tn), lambda i, j: (0, j)),
                  pl.BlockSpec((kg, tn), lambda i, j: (1, j)),
                  pl.BlockSpec((tm, tn), lambda i, j: (i, j))],
        out_specs=pl.BlockSpec((tm, tn), lambda i, j: (i, j)),
        out_shape=jax.ShapeDtypeStruct((m, n), F32),
        compiler_params=_params("parallel", "arbitrary"),
        name="outproj",
    )(a, g, w, w, x)


def _up_body(x_ref, g_ref, h0_ref, wg_ref, wv_ref, cwg_ref, cwv_ref, cbg_ref, cbv_ref, wd_ref,
             o_ref, wdo_ref, h_a, h_b, h_stage, ext_a, ext_b, carry_ref, *, tm, nj, n_tiles, blocks_per_seq):
    t = pl.program_id(0)
    te = jnp.maximum(t - 1, 0)
    je = te % nj
    seq_start = ((te // nj) % blocks_per_seq) == 0
    tmm = jnp.minimum(t, n_tiles - 1)
    cur = (tmm // nj) % 2
    jn = jnp.minimum(tmm % nj, tm // NORM_ROWS - 1)

    @pl.when(t == 0)
    def _():
        ext_a[...] = jnp.zeros_like(ext_a)
        ext_b[...] = jnp.zeros_like(ext_b)
        carry_ref[...] = jnp.zeros_like(carry_ref)
        h_a[...] = h0_ref[...]

    def stage(ext_w, ext_r, h_cur):
        for half in range(2):
            slot = half * nj + je
            prev = jnp.where(seq_start, 0.0, carry_ref[slot])
            carry_ref[slot] = ext_r[half, tm:tm + CARRY_ROWS, :]
            ext_r[half, 0:CARRY_ROWS, :] = prev

        def conv(half, r0, cw_ref, cb_ref):
            slab = ext_r[half, r0:r0 + CARRY_ROWS + EPI_ROWS, :]
            y = cb_ref[...]
            for j in range(CONV_WIDTH - 1, 0, -1):
                tap = pltpu.roll(slab, j, axis=0)[CARRY_ROWS:, :]
                y = y + cw_ref[CONV_WIDTH - 1 - j:CONV_WIDTH - j, :] * tap
            return y + cw_ref[CONV_WIDTH - 1:CONV_WIDTH, :] * slab[CARRY_ROWS:, :]

        kc = h_cur.shape[1] // K_SPLIT
        n_chunks = tm // EPI_ROWS
        chains = ((0, 0, tm), (1, 0, tm - LAST_CHAIN_ROWS), (1, tm - LAST_CHAIN_ROWS, LAST_CHAIN_ROWS))
        work_done, chunk, order_zero = 0, 0, None
        for ci, (half, row0, nrows) in enumerate(chains):
            w_ref = (wg_ref, wv_ref)[half]
            acc = order_zero if ci == len(chains) - 1 else None
            for p in range(K_SPLIT):
                ks = slice(p * kc, (p + 1) * kc)
                part = jnp.dot(h_cur[row0:row0 + nrows, ks], w_ref[ks, :].astype(BF16),
                               preferred_element_type=F32)
                acc = part if acc is None else acc + part
                work_done += nrows
                while chunk < min(n_chunks, (n_chunks * work_done) // ((2 * tm - LAST_CHAIN_ROWS) * K_SPLIT)):
                    r0 = chunk * EPI_ROWS
                    gate = conv(0, r0, cwg_ref, cbg_ref)
                    val = conv(1, r0, cwv_ref, cbv_ref)
                    o_ref[r0:r0 + EPI_ROWS, :] = (jax.nn.silu(gate) * val).astype(o_ref.dtype)
                    chunk += 1
                if ci == 0 and p == K_SPLIT // 2:
                    wdo_ref[...] = wd_ref[...].astype(wdo_ref.dtype)
            ext_w[half, CARRY_ROWS + row0:CARRY_ROWS + row0 + nrows, :] = acc
            if ci == 0:
                xs = x_ref[...]
                width = xs.shape[1]
                part = None
                for c0 in range(0, width, HEAD_DIM):
                    sq = xs[:, c0:c0 + HEAD_DIM] * xs[:, c0:c0 + HEAD_DIM]
                    part = sq if part is None else part + sq
                hi = part.astype(BF16)
                lo = (part - hi.astype(F32)).astype(BF16)
                ones = jnp.ones((HEAD_DIM, HEAD_DIM), BF16)
                ssq = (jnp.dot(hi, ones, preferred_element_type=F32)
                       + jnp.dot(lo, ones, preferred_element_type=F32))
                rs = lax.rsqrt(ssq * (1.0 / width) + EPS)
                mix = None
                for c0 in range(0, width, HEAD_DIM):
                    cols = slice(c0, c0 + HEAD_DIM)
                    y = xs[:, cols] * rs * g_ref[:, cols]
                    h_stage[:, cols] = y.astype(h_stage.dtype)
                    mix = y[0:8, :] if mix is None else mix + y[0:8, :]
                bits = lax.bitcast_convert_type(mix, jnp.uint32)
                z8 = lax.shift_right_logical(lax.shift_right_logical(bits, jnp.uint32(16)),
                                             jnp.uint32(16)).astype(F32)
                order_zero = jnp.tile(z8, (LAST_CHAIN_ROWS // 8, wg_ref.shape[1] // HEAD_DIM))

    for tp, (ext_w, ext_r) in enumerate(((ext_a, ext_b), (ext_b, ext_a))):
        for hp, h_cur in enumerate((h_a, h_b)):
            pl.when((t % 2 == tp) & (cur == hp))(functools.partial(stage, ext_w, ext_r, h_cur))

    rows = pl.ds(pl.multiple_of(jn * NORM_ROWS, NORM_ROWS), NORM_ROWS)
    for hp, h_next in enumerate((h_b, h_a)):
        @pl.when(cur == hp)
        def _(h_next=h_next):
            h_next[rows, :] = h_stage[...]


def _upproj(x, g, w, conv_w, conv_b, w_down, layer, seq, tm=1024, tn=256):
    m, k = x.shape
    nj = D_FF // tn
    n_tiles = (m // tm) * nj
    n_down = w_down.shape[2]
    slab = D_FF // n_tiles
    assert slab * n_tiles == D_FF and slab % 16 == 0
    slabs_per_block = tm // NORM_ROWS
    assert slabs_per_block <= nj
    n_slabs = m // NORM_ROWS
    h0 = _rmsnorm(x, g, BF16, rows=tm)
    body = functools.partial(_up_body, tm=tm, nj=nj, n_tiles=n_tiles, blocks_per_seq=seq // tm)
    mm = lambda t: jnp.minimum(t, n_tiles - 1)
    ep = lambda t: jnp.maximum(t - 1, 0)

    def next_slab(t):
        s = (mm(t) // nj + 1) * slabs_per_block + jnp.minimum(mm(t) % nj, slabs_per_block - 1)
        return (jnp.minimum(s, n_slabs - 1), 0)

    return pl.pallas_call(
        body,
        grid=(n_tiles + 1,),
        in_specs=[pl.BlockSpec((NORM_ROWS, k), next_slab),
                  pl.BlockSpec((1, k), lambda t: (0, 0)),
                  pl.BlockSpec((tm, k), lambda t: (0, 0), pipeline_mode=pl.Buffered(1)),
                  pl.BlockSpec((None, k, tn), lambda t: (layer, 0, mm(t) % nj)),
                  pl.BlockSpec((None, k, tn), lambda t: (layer, 0, mm(t) % nj + nj)),
                  pl.BlockSpec((None, CONV_WIDTH, tn), lambda t: (layer, 0, ep(t) % nj)),
                  pl.BlockSpec((None, CONV_WIDTH, tn), lambda t: (layer, 0, ep(t) % nj + nj)),
                  pl.BlockSpec((None, 1, tn), lambda t: (layer, 0, ep(t) % nj)),
                  pl.BlockSpec((None, 1, tn), lambda t: (layer, 0, ep(t) % nj + nj)),
                  pl.BlockSpec((None, slab, n_down), lambda t: (layer, mm(t), 0))],
        out_specs=[pl.BlockSpec((tm, tn), lambda t: (ep(t) // nj, ep(t) % nj)),
                   pl.BlockSpec((slab, n_down), lambda t: (mm(t), 0))],
        out_shape=[jax.ShapeDtypeStruct((m, D_FF), BF16),
                   jax.ShapeDtypeStruct((D_FF, n_down), BF16)],
        scratch_shapes=[pltpu.VMEM((tm, k), BF16),
                        pltpu.VMEM((tm, k), BF16),
                        pltpu.VMEM((NORM_ROWS, k), BF16),
                        pltpu.VMEM((2, tm + CARRY_ROWS, tn), F32),
                        pltpu.VMEM((2, tm + CARRY_ROWS, tn), F32),
                        pltpu.VMEM((2 * nj, CARRY_ROWS, tn), F32)],
        compiler_params=_params("arbitrary"),
        name="upproj_conv_gate",
    )(x, g.reshape(1, k), h0, w, w, conv_w, conv_w, conv_b, conv_b, w_down)


def _down_body(a_ref, w_ref, x_ref, o_ref):
    o_ref[...] = x_ref[...] + jnp.dot(a_ref[...], w_ref[...], preferred_element_type=F32)


def _downproj(a, w, x, tm=512, tn=512):
    m, k = a.shape
    n = w.shape[1]
    return pl.pallas_call(
        _down_body,
        grid=(m // tm, n // tn),
        in_specs=[pl.BlockSpec((tm, k), lambda i, j: (i, 0)),
                  pl.BlockSpec((k, tn), lambda i, j: (0, j)),
                  pl.BlockSpec((tm, tn), lambda i, j: (i, j))],
        out_specs=pl.BlockSpec((tm, tn), lambda i, j: (i, j)),
        out_shape=jax.ShapeDtypeStruct((m, n), F32),
        compiler_params=_params("parallel", "arbitrary"),
        name="downproj",
    )(a, w, x)


def kernel(x, attn_norm_g, w_in, sgu_norm_g, w_spatial, b_spatial, mix_norm_g, w_out, ffn_norm_g,
           w_up, conv_w, conv_b, w_down, final_norm_g):
    b, s, d = x.shape
    depth = w_in.shape[0]
    xm = x.reshape(b * s, d)
    slopes = 2.0 ** (-8.0 * jnp.arange(1, N_HEADS + 1, dtype=F32) / N_HEADS)
    slopes = jnp.broadcast_to(slopes[:, None, None], (N_HEADS, 1, HEAD_DIM))
    conv_b3 = conv_b.reshape(depth, 1, 2 * D_FF)
    for i in range(depth):
        mix_g = mix_norm_g[i].reshape(1, D_MODEL)
        h = _rmsnorm(xm, attn_norm_g[i], BF16)
        z = _inproj(h, w_in, i)
        a = _attention(z.reshape(b, s, N_IN), slopes, mix_g).reshape(b * s, D_ATTN)
        g, w_out_bf = _sgu(z, sgu_norm_g[i], w_spatial[i], b_spatial[i].T, mix_g, w_out, i)
        xm = _outproj(a, g, w_out_bf, xm)
        act, w_down_bf = _upproj(xm, ffn_norm_g[i], w_up, conv_w, conv_b3, w_down, i, s)
        xm = _downproj(act, w_down_bf, xm)
    return _rmsnorm(xm, final_norm_g, F32).reshape(b, s, d)
```

```python
import functools
import math

import jax
import jax.numpy as jnp
from jax import lax
from jax.experimental import pallas as pl
from jax.experimental.pallas import tpu as pltpu

F32 = jnp.float32
BF16 = jnp.bfloat16

D_MODEL = 4096
HEAD_DIM = 128
D_ATTN = D_MODEL // 2
D_SGU = D_MODEL - D_ATTN
N_HEADS = D_ATTN // HEAD_DIM
N_GROUPS = D_SGU // HEAD_DIM
CHUNK = 128
BLOCK = 128
DILATIONS = (1, 4, 16)
N_IN = 3 * D_ATTN + 2 * D_SGU
D_FF = 256 * ((8 * D_MODEL // 3 + 255) // 256)
CONV_WIDTH = 3
EPS = 1e-6
LOG2E = math.log2(math.e)
VMEM_LIMIT = 61 * 1024 * 1024
CARRY_ROWS = 8
EPI_ROWS = 32
K_SPLIT = 16
NORM_ROWS = 32
SGU_BATCH = 4
ATT_GROUP = 32


def _params(*sem):
    return pltpu.CompilerParams(dimension_semantics=sem, vmem_limit_bytes=VMEM_LIMIT)


def _rmsnorm_body(x_ref, g_ref, o_ref):
    x = x_ref[...]
    y = x * lax.rsqrt(jnp.mean(x * x, axis=-1, keepdims=True) + EPS)
    o_ref[...] = (y * g_ref[...]).astype(o_ref.dtype)


def _rmsnorm(x, g, out_dtype, tm=512, rows=None):
    m, d = x.shape
    m = m if rows is None else rows
    return pl.pallas_call(
        _rmsnorm_body,
        grid=(m // tm,),
        in_specs=[pl.BlockSpec((tm, d), lambda i: (i, 0)),
                  pl.BlockSpec((1, d), lambda i: (0, 0))],
        out_specs=pl.BlockSpec((tm, d), lambda i: (i, 0)),
        out_shape=jax.ShapeDtypeStruct((m, d), out_dtype),
        compiler_params=_params("parallel"),
        name="rmsnorm",
    )(x, g.reshape(1, d))


def _inproj_body(h_ref, w_ref, o_ref):
    w = w_ref[...].astype(BF16)
    o_ref[...] = jnp.dot(h_ref[...], w, preferred_element_type=F32).astype(o_ref.dtype)


def _inproj(h, w, layer, tm=1024, tn=1024):
    m, k = h.shape
    n = w.shape[2]
    return pl.pallas_call(
        _inproj_body,
        grid=(m // tm, n // tn),
        in_specs=[pl.BlockSpec((tm, k), lambda i, j: (i, 0)),
                  pl.BlockSpec((None, k, tn), lambda i, j: (layer, 0, j))],
        out_specs=pl.BlockSpec((tm, tn), lambda i, j: (i, j)),
        out_shape=jax.ShapeDtypeStruct((m, n), BF16),
        compiler_params=_params("parallel", "arbitrary"),
        name="inproj",
    )(h, w)


def _aligned(i, m):
    return i if isinstance(i, int) else pl.multiple_of(i, m)


def _attn_body(q_ref, k_ref, v_ref, slope_ref, g_ref, o_ref, xf, x4f, qd, kd, vd, bias2, ob, mb, lb,
               *, seq):
    qscale = HEAD_DIM ** -0.5 * LOG2E
    d1, d2 = DILATIONS[1], DILATIONS[2]
    assert d2 == d1 * d1 and DILATIONS[0] == 1
    sub1, sub2 = seq // d1, seq // d2
    for src, dst, mul, first_c in ((q_ref, qd, qscale, 0), (k_ref, kd, None, 1), (v_ref, vd, None, 1)):
        if first_c == 0:
            xf[...] = src[...].astype(F32) * mul
            dst[0] = xf[...].astype(BF16)
        else:
            xf[...] = src[...].astype(F32)

        def deint1(r, carry, dst=dst, slot=1 - first_c):
            rows = pl.ds(pl.multiple_of(r * sub1, sub1), sub1)
            x = xf[pl.ds(r, sub1, stride=d1), :]
            x4f[rows, :] = x
            dst[slot, rows, :] = x.astype(BF16)
            return carry

        lax.fori_loop(0, d1, deint1, 0)

        def deint2(r, carry, dst=dst, slot=2 - first_c):
            rows = pl.ds(pl.multiple_of(r * sub2, sub2), sub2)
            x = x4f[pl.ds((r % d1) * sub1 + r // d1, sub2, stride=d1), :]
            dst[slot, rows, :] = x.astype(BF16)
            return carry

        lax.fori_loop(0, d2, deint2, 0)

    slope = slope_ref[...]
    qi = lax.broadcasted_iota(jnp.int32, (BLOCK, 2 * BLOCK), 0)
    kj = lax.broadcasted_iota(jnp.int32, (BLOCK, 2 * BLOCK), 1)
    dist = qi + BLOCK - kj
    valid = (dist >= 0) & (dist <= BLOCK)
    for c, d in enumerate(DILATIONS):
        bias2[c] = jnp.where(valid, (-LOG2E * slope[:, :1]) * (dist * d).astype(F32), -jnp.inf)
    ones = jnp.ones((2 * BLOCK, HEAD_DIM), BF16)

    def group(c, d, blocks):
        getk = (lambda rows: k_ref[rows, :]) if c == 0 else (lambda rows: kd[c - 1, rows, :])
        getv = (lambda rows: v_ref[rows, :]) if c == 0 else (lambda rows: vd[c - 1, rows, :])
        qs, ks, vs = [], [], []
        for gi, _, first in blocks:
            qrows = pl.ds(_aligned(gi * BLOCK, BLOCK), BLOCK)
            krows = qrows if first else pl.ds(_aligned(gi * BLOCK - BLOCK, BLOCK), 2 * BLOCK)
            qs.append(qd[c, qrows, :])
            ks.append(getk(krows))
            v = getv(krows)
            vs.append(jnp.concatenate([v, ones[:v.shape[0]]], axis=1))
        ss = [lax.dot_general(q, k, (((1,), (1,)), ((), ())), preferred_element_type=F32)
              for q, k in zip(qs, ks)]
        ps, ms = [], []
        for s, (_, _, first) in zip(ss, blocks):
            s = s + (bias2[c, :, BLOCK:] if first else bias2[c])
            mx = jnp.max(s, axis=-1, keepdims=True)
            ms.append(mx)
            ps.append(jnp.exp2(s - mx).astype(BF16))
        os = [jnp.dot(p, v, preferred_element_type=F32) for p, v in zip(ps, vs)]
        for o, mx, (_, rows, _) in zip(os, ms, blocks):
            ob[c, rows, :] = o[:, :HEAD_DIM]
            lb[c, rows, :] = o[:, HEAD_DIM:]
            mb[c, rows, :] = jnp.broadcast_to(mx, (BLOCK, HEAD_DIM))

    for c, d in enumerate(DILATIONS):
        nb = seq // d // BLOCK

        def blk(r, n, first, d=d, nb=nb):
            gi = r * nb + n
            if d <= d1:
                return (gi, pl.ds(_aligned(gi * BLOCK, BLOCK), BLOCK), first)
            q = d // d1
            start = (r % d1) * sub1 + q * BLOCK * n + r // d1
            return (gi, pl.ds(start, BLOCK, stride=q), first)

        if nb >= ATT_GROUP:
            def r_body(r, carry, c=c, d=d, nb=nb, blk=blk):
                group(c, d, [blk(r, n, n == 0) for n in range(ATT_GROUP)])

                def n_body(i, carry2):
                    group(c, d, [blk(r, i * ATT_GROUP + t, False) for t in range(ATT_GROUP)])
                    return carry2

                return lax.fori_loop(1, nb // ATT_GROUP, n_body, carry)

            if d == 1:
                r_body(0, 0)
            else:
                lax.fori_loop(0, d, r_body, 0)
        else:
            per = ATT_GROUP // nb

            def r_body(i, carry, c=c, d=d, nb=nb, blk=blk, per=per):
                group(c, d, [blk(i * per + t, n, n == 0) for t in range(per) for n in range(nb)])
                return carry

            lax.fori_loop(0, d // per, r_body, 0)

    g = g_ref[...]
    step, piece = 512, 64

    def fin(t, carry):
        r = (t * step) // sub1
        m0 = (t * step) % sub1
        for p0 in range(0, step, piece):
            rows = pl.ds(pl.multiple_of(t * step + p0, piece), piece)
            toks = pl.ds(r + d1 * (m0 + p0), piece, stride=d1)
            ma, mb1, mb2 = mb[0, toks, :], mb[1, rows, :], mb[2, rows, :]
            mx = jnp.maximum(jnp.maximum(ma, mb1), mb2)
            e0, e1, e2 = jnp.exp2(ma - mx), jnp.exp2(mb1 - mx), jnp.exp2(mb2 - mx)
            den = e0 * lb[0, toks, :] + e1 * lb[1, rows, :] + e2 * lb[2, rows, :]
            num = e0 * ob[0, toks, :] + e1 * ob[1, rows, :] + e2 * ob[2, rows, :]
            msq = jnp.mean(num * num, axis=-1, keepdims=True)
            xf[toks, :] = num * lax.rsqrt(msq + EPS * (den * den)) * g
        return carry

    lax.fori_loop(0, seq // step, fin, 0)
    o_ref[...] = xf[...].astype(o_ref.dtype)


def _attention(z3, slopes, mix_g):
    b, s, _ = z3.shape
    blk = lambda off: pl.BlockSpec((None, s, HEAD_DIM), lambda bi, h, off=off: (bi, 0, off + h))
    return pl.pallas_call(
        functools.partial(_attn_body, seq=s),
        grid=(b, N_HEADS),
        in_specs=[blk(0), blk(N_HEADS), blk(2 * N_HEADS),
                  pl.BlockSpec((None, 1, HEAD_DIM), lambda bi, h: (h, 0, 0)),
                  pl.BlockSpec((1, HEAD_DIM), lambda bi, h: (0, h))],
        out_specs=pl.BlockSpec((None, s, HEAD_DIM), lambda bi, h: (bi, 0, h)),
        out_shape=jax.ShapeDtypeStruct((b, s, D_ATTN), BF16),
        scratch_shapes=[pltpu.VMEM((s, HEAD_DIM), F32)] * 2
        + [pltpu.VMEM((len(DILATIONS), s, HEAD_DIM), BF16)]
        + [pltpu.VMEM((len(DILATIONS) - 1, s, HEAD_DIM), BF16)] * 2
        + [pltpu.VMEM((len(DILATIONS), BLOCK, 2 * BLOCK), F32)]
        + [pltpu.VMEM((len(DILATIONS), s, HEAD_DIM), F32)] * 3,
        compiler_params=_params("parallel", "parallel"),
        name="dilated_attn",
    )(z3, z3, z3, slopes, mix_g)


def _gelu(x):
    return 0.5 * x * (1.0 + lax.erf(x * math.sqrt(0.5)))


def _sgu_body(u_ref, v_ref, ng_ref, w_ref, bt_ref, mg_ref, wo_ref, o_ref, wob_ref, *, tm):
    wob_ref[...] = wo_ref[...].astype(wob_ref.dtype)
    ti = lax.broadcasted_iota(jnp.int32, (CHUNK, CHUNK), 0)
    si = lax.broadcasted_iota(jnp.int32, (CHUNK, CHUNK), 1)
    causal = si <= ti
    n_ch = tm // CHUNK
    for g0 in range(0, N_GROUPS, SGU_BATCH):
        groups = range(g0, g0 + SGU_BATCH)
        vcat = []
        for g in groups:
            cols = slice(g * HEAD_DIM, (g + 1) * HEAD_DIM)
            ng = ng_ref[:, cols]
            vs = []
            for ch in range(n_ch):
                v = _gelu(v_ref[ch * CHUNK:(ch + 1) * CHUNK, cols].astype(F32))
                v = v * lax.rsqrt(jnp.mean(v * v, axis=-1, keepdims=True) + EPS) * ng
                vs.append(v.astype(BF16))
            vcat.append(jnp.concatenate(vs, axis=1))
        mixed = [jnp.dot(jnp.where(causal, w_ref[g], 0.0).astype(BF16), vc, preferred_element_type=F32)
                 for g, vc in zip(groups, vcat)]
        for g, mx in zip(groups, mixed):
            cols = slice(g * HEAD_DIM, (g + 1) * HEAD_DIM)
            bcol = bt_ref[:, g:g + 1]
            mg = mg_ref[:, cols]
            for ch in range(n_ch):
                rows = slice(ch * CHUNK, (ch + 1) * CHUNK)
                u = _gelu(u_ref[rows, cols].astype(F32))
                y = u * (mx[:, ch * HEAD_DIM:(ch + 1) * HEAD_DIM] + bcol)
                y = y * lax.rsqrt(jnp.mean(y * y, axis=-1, keepdims=True) + EPS) * mg
                o_ref[rows, cols] = y.astype(o_ref.dtype)


def _sgu(z, sgu_norm_g, w_spatial, b_spatial_t, mix_g, w_out, layer, tm=1024):
    m = z.shape[0]
    steps = m // tm
    k_out, n_out = w_out.shape[1:]
    slab = k_out // steps
    assert slab * steps == k_out and slab % 16 == 0
    ublk = 3 * D_ATTN // D_SGU
    return pl.pallas_call(
        functools.partial(_sgu_body, tm=tm),
        grid=(steps,),
        in_specs=[pl.BlockSpec((tm, D_SGU), lambda i: (i, ublk)),
                  pl.BlockSpec((tm, D_SGU), lambda i: (i, ublk + 1)),
                  pl.BlockSpec((1, D_SGU), lambda i: (0, 0)),
                  pl.BlockSpec((N_GROUPS, CHUNK, CHUNK), lambda i: (0, 0, 0)),
                  pl.BlockSpec((CHUNK, N_GROUPS), lambda i: (0, 0)),
                  pl.BlockSpec((1, D_SGU), lambda i: (0, 1)),
                  pl.BlockSpec((None, slab, n_out), lambda i: (layer, i, 0))],
        out_specs=[pl.BlockSpec((tm, D_SGU), lambda i: (i, 0)),
                   pl.BlockSpec((slab, n_out), lambda i: (i, 0))],
        out_shape=[jax.ShapeDtypeStruct((m, D_SGU), BF16),
                   jax.ShapeDtypeStruct((k_out, n_out), BF16)],
        compiler_params=_params("parallel"),
        name="sgu",
    )(z, z, sgu_norm_g.reshape(1, D_SGU), w_spatial, b_spatial_t, mix_g, w_out)


def _outproj_body(a_ref, g_ref, wa_ref, wg_ref, x_ref, o_ref):
    acc = jnp.dot(a_ref[...], wa_ref[...], preferred_element_type=F32)
    acc = acc + jnp.dot(g_ref[...], wg_ref[...], preferred_element_type=F32)
    o_ref[...] = x_ref[...] + acc


def _outproj(a, g, w, x, tm=1024, tn=1024):
    m, ka = a.shape
    kg = g.shape[1]
    assert ka == kg
    n = w.shape[1]
    return pl.pallas_call(
        _outproj_body,
        grid=(m // tm, n // tn),
        in_specs=[pl.BlockSpec((tm, ka), lambda i, j: (i, 0)),
                  pl.BlockSpec((tm, kg), lambda i, j: (i, 0)),
                  pl.BlockSpec((ka, tn), lambda i, j: (0, j)),
                  pl.BlockSpec((kg, tn), lambda i, j: (1, j)),
                  pl.BlockSpec((tm, tn), lambda i, j: (i, j))],
        out_specs=pl.BlockSpec((tm, tn), lambda i, j: (i, j)),
        out_shape=jax.ShapeDtypeStruct((m, n), F32),
        compiler_params=_params("parallel", "arbitrary"),
        name="outproj",
    )(a, g, w, w, x)


def _up_body(x_ref, g_ref, h0_ref, wg_ref, wv_ref, cwg_ref, cwv_ref, cbg_ref, cbv_ref, wd_ref,
             o_ref, wdo_ref, h_a, h_b, h_stage, ext_a, ext_b, carry_ref, *, tm, nj, n_tiles, blocks_per_seq):
    t = pl.program_id(0)
    te = jnp.maximum(t - 1, 0)
    je = te % nj
    seq_start = ((te // nj) % blocks_per_seq) == 0
    tmm = jnp.minimum(t, n_tiles - 1)
    cur = (tmm // nj) % 2
    jn = jnp.minimum(tmm % nj, tm // NORM_ROWS - 1)

    @pl.when(t == 0)
    def _():
        ext_a[...] = jnp.zeros_like(ext_a)
        ext_b[...] = jnp.zeros_like(ext_b)
        carry_ref[...] = jnp.zeros_like(carry_ref)
        h_a[...] = h0_ref[...]

    def stage(ext_w, ext_r, h_cur):
        for half in range(2):
            slot = half * nj + je
            prev = jnp.where(seq_start, 0.0, carry_ref[slot])
            carry_ref[slot] = ext_r[half, tm:tm + CARRY_ROWS, :]
            ext_r[half, 0:CARRY_ROWS, :] = prev

        def conv(half, r0, cw_ref, cb_ref):
            slab = ext_r[half, r0:r0 + CARRY_ROWS + EPI_ROWS, :]
            y = cb_ref[...]
            for j in range(CONV_WIDTH - 1, 0, -1):
                tap = pltpu.roll(slab, j, axis=0)[CARRY_ROWS:, :]
                y = y + cw_ref[CONV_WIDTH - 1 - j:CONV_WIDTH - j, :] * tap
            return y + cw_ref[CONV_WIDTH - 1:CONV_WIDTH, :] * slab[CARRY_ROWS:, :]

        kc = h_cur.shape[1] // K_SPLIT
        rows_per_piece = tm // (2 * K_SPLIT)
        piece = 0
        for half, w_ref in enumerate((wg_ref, wv_ref)):
            acc = None
            for p in range(K_SPLIT):
                ks = slice(p * kc, (p + 1) * kc)
                part = jnp.dot(h_cur[:, ks], w_ref[ks, :].astype(BF16), preferred_element_type=F32)
                acc = part if acc is None else acc + part
                for r0 in range(piece * rows_per_piece, (piece + 1) * rows_per_piece, EPI_ROWS):
                    gate = conv(0, r0, cwg_ref, cbg_ref)
                    val = conv(1, r0, cwv_ref, cbv_ref)
                    o_ref[r0:r0 + EPI_ROWS, :] = (jax.nn.silu(gate) * val).astype(o_ref.dtype)
                piece += 1
                if piece == K_SPLIT // 2:
                    wdo_ref[...] = wd_ref[...].astype(wdo_ref.dtype)
            ext_w[half, CARRY_ROWS:, :] = acc
            if half == 0:
                xs = x_ref[...]
                width = xs.shape[1]
                part = None
                for c0 in range(0, width, HEAD_DIM):
                    sq = xs[:, c0:c0 + HEAD_DIM] * xs[:, c0:c0 + HEAD_DIM]
                    part = sq if part is None else part + sq
                hi = part.astype(BF16)
                lo = (part - hi.astype(F32)).astype(BF16)
                ones = jnp.ones((HEAD_DIM, HEAD_DIM), BF16)
                ssq = (jnp.dot(hi, ones, preferred_element_type=F32)
                       + jnp.dot(lo, ones, preferred_element_type=F32))
                rs = lax.rsqrt(ssq * (1.0 / width) + EPS)
                for c0 in range(0, width, HEAD_DIM):
                    cols = slice(c0, c0 + HEAD_DIM)
                    h_stage[:, cols] = (xs[:, cols] * rs * g_ref[:, cols]).astype(h_stage.dtype)

    for tp, (ext_w, ext_r) in enumerate(((ext_a, ext_b), (ext_b, ext_a))):
        for hp, h_cur in enumerate((h_a, h_b)):
            pl.when((t % 2 == tp) & (cur == hp))(functools.partial(stage, ext_w, ext_r, h_cur))

    rows = pl.ds(pl.multiple_of(jn * NORM_ROWS, NORM_ROWS), NORM_ROWS)
    for hp, h_next in enumerate((h_b, h_a)):
        @pl.when(cur == hp)
        def _(h_next=h_next):
            h_next[rows, :] = h_stage[...]


def _upproj(x, g, w, conv_w, conv_b, w_down, layer, seq, tm=1024, tn=256):
    m, k = x.shape
    nj = D_FF // tn
    n_tiles = (m // tm) * nj
    n_down = w_down.shape[2]
    slab = D_FF // n_tiles
    assert slab * n_tiles == D_FF and slab % 16 == 0
    slabs_per_block = tm // NORM_ROWS
    assert slabs_per_block <= nj
    n_slabs = m // NORM_ROWS
    h0 = _rmsnorm(x, g, BF16, rows=tm)
    body = functools.partial(_up_body, tm=tm, nj=nj, n_tiles=n_tiles, blocks_per_seq=seq // tm)
    mm = lambda t: jnp.minimum(t, n_tiles - 1)
    ep = lambda t: jnp.maximum(t - 1, 0)

    def next_slab(t):
        s = (mm(t) // nj + 1) * slabs_per_block + jnp.minimum(mm(t) % nj, slabs_per_block - 1)
        return (jnp.minimum(s, n_slabs - 1), 0)

    return pl.pallas_call(
        body,
        grid=(n_tiles + 1,),
        in_specs=[pl.BlockSpec((NORM_ROWS, k), next_slab),
                  pl.BlockSpec((1, k), lambda t: (0, 0)),
                  pl.BlockSpec((tm, k), lambda t: (0, 0), pipeline_mode=pl.Buffered(1)),
                  pl.BlockSpec((None, k, tn), lambda t: (layer, 0, mm(t) % nj)),
                  pl.BlockSpec((None, k, tn), lambda t: (layer, 0, mm(t) % nj + nj)),
                  pl.BlockSpec((None, CONV_WIDTH, tn), lambda t: (layer, 0, ep(t) % nj)),
                  pl.BlockSpec((None, CONV_WIDTH, tn), lambda t: (layer, 0, ep(t) % nj + nj)),
                  pl.BlockSpec((None, 1, tn), lambda t: (layer, 0, ep(t) % nj)),
                  pl.BlockSpec((None, 1, tn), lambda t: (layer, 0, ep(t) % nj + nj)),
                  pl.BlockSpec((None, slab, n_down), lambda t: (layer, mm(t), 0))],
        out_specs=[pl.BlockSpec((tm, tn), lambda t: (ep(t) // nj, ep(t) % nj)),
                   pl.BlockSpec((slab, n_down), lambda t: (mm(t), 0))],
        out_shape=[jax.ShapeDtypeStruct((m, D_FF), BF16),
                   jax.ShapeDtypeStruct((D_FF, n_down), BF16)],
        scratch_shapes=[pltpu.VMEM((tm, k), BF16),
                        pltpu.VMEM((tm, k), BF16),
                        pltpu.VMEM((NORM_ROWS, k), BF16),
                        pltpu.VMEM((2, tm + CARRY_ROWS, tn), F32),
                        pltpu.VMEM((2, tm + CARRY_ROWS, tn), F32),
                        pltpu.VMEM((2 * nj, CARRY_ROWS, tn), F32)],
        compiler_params=_params("arbitrary"),
        name="upproj_conv_gate",
    )(x, g.reshape(1, k), h0, w, w, conv_w, conv_w, conv_b, conv_b, w_down)


def _down_body(a_ref, w_ref, x_ref, o_ref):
    o_ref[...] = x_ref[...] + jnp.dot(a_ref[...], w_ref[...], preferred_element_type=F32)


def _downproj(a, w, x, tm=512, tn=512):
    m, k = a.shape
    n = w.shape[1]
    return pl.pallas_call(
        _down_body,
        grid=(m // tm, n // tn),
        in_specs=[pl.BlockSpec((tm, k), lambda i, j: (i, 0)),
                  pl.BlockSpec((k, tn), lambda i, j: (0, j)),
                  pl.BlockSpec((tm, tn), lambda i, j: (i, j))],
        out_specs=pl.BlockSpec((tm, tn), lambda i, j: (i, j)),
        out_shape=jax.ShapeDtypeStruct((m, n), F32),
        compiler_params=_params("parallel", "arbitrary"),
        name="downproj",
    )(a, w, x)


def kernel(x, attn_norm_g, w_in, sgu_norm_g, w_spatial, b_spatial, mix_norm_g, w_out, ffn_norm_g,
           w_up, conv_w, conv_b, w_down, final_norm_g):
    b, s, d = x.shape
    depth = w_in.shape[0]
    xm = x.reshape(b * s, d)
    slopes = 2.0 ** (-8.0 * jnp.arange(1, N_HEADS + 1, dtype=F32) / N_HEADS)
    slopes = jnp.broadcast_to(slopes[:, None, None], (N_HEADS, 1, HEAD_DIM))
    conv_b3 = conv_b.reshape(depth, 1, 2 * D_FF)
    for i in range(depth):
        mix_g = mix_norm_g[i].reshape(1, D_MODEL)
        h = _rmsnorm(xm, attn_norm_g[i], BF16)
        z = _inproj(h, w_in, i)
        a = _attention(z.reshape(b, s, N_IN), slopes, mix_g).reshape(b * s, D_ATTN)
        g, w_out_bf = _sgu(z, sgu_norm_g[i], w_spatial[i], b_spatial[i].T, mix_g, w_out, i)
        xm = _outproj(a, g, w_out_bf, xm)
        act, w_down_bf = _upproj(xm, ffn_norm_g[i], w_up, conv_w, conv_b3, w_down, i, s)
        xm = _downproj(act, w_down_bf, xm)
    return _rmsnorm(xm, final_norm_g, F32).reshape(b, s, d)
```
